```python
import jax, jax.numpy as jnp
from jax import lax
import numpy as np

D_MODEL = 2048
BATCH = 4
SEQ = 4096
DEPTH = 4

D_MIX = D_MODEL
N_MIXERS = 4
D_GROUP = D_MIX // N_MIXERS
HEAD_DIM = 128
N_HEADS = D_GROUP // HEAD_DIM
CHUNK = 128
CONV_WIDTH = 3
POOL_WINDOWS = (2, 4, 8, 16)
N_POOL = len(POOL_WINDOWS)
D_IN_PROJ = D_GROUP + 2 * D_GROUP + 3 * D_GROUP + D_GROUP
N_EXPERT_GROUPS = 4
EXPERTS_PER_GROUP = 8
N_EXPERTS = N_EXPERT_GROUPS * EXPERTS_PER_GROUP
TOP_K = 2
D_EXPERT = 768
MOE_BLOCK = 128
EPS = 1e-6

kernel_name = "hybrid_fourier_sgu_conv_pool_hmoe_encoder"


def rmsnorm(x, g):
    x32 = x.astype(jnp.float32)
    y = x32 * lax.rsqrt(jnp.mean(x32 * x32, axis=-1, keepdims=True) + EPS)
    return (y * g.astype(jnp.float32)).astype(x.dtype)


def fourier_mixer(z, w):
    b, s, _ = z.shape
    zh = z.astype(jnp.float32).reshape(b, s, N_HEADS, HEAD_DIM)
    f = jnp.fft.fft2(zh, axes=(1, 3), norm="ortho").real
    return f.reshape(b, s, D_GROUP).astype(z.dtype) @ w


def spatial_gating_mixer(z, w_s, b_s):
    b, s, _ = z.shape
    u, v = jnp.split(jax.nn.gelu(z, approximate=False), 2, axis=-1)
    v = v.reshape(b, s // CHUNK, CHUNK, N_HEADS, HEAD_DIM)
    mixed = jnp.einsum("hpq,bnqhc->bnphc", w_s, v) + b_s.T[None, None, :, :, None]
    return u * mixed.reshape(b, s, D_GROUP)


def short_conv_mixer(z, conv_w):
    gate_b, gate_c, val = jnp.split(z, 3, axis=-1)
    a = gate_c * val
    conv = lax.conv_general_dilated(
        a, conv_w[:, None, :], window_strides=(1,), padding=((1, 1),),
        dimension_numbers=("NWC", "WIO", "NWC"), feature_group_count=D_GROUP)
    return gate_b * conv


def multiscale_pool_mixer(z, w_pool, scale):
    b, s, _ = z.shape
    z32 = z.astype(jnp.float32)
    cs = jnp.concatenate([jnp.zeros((b, 1, D_GROUP), jnp.float32),
                          jnp.cumsum(z32, axis=1)], axis=1)
    t = jnp.arange(s)
    outs = []
    for g, w in enumerate(POOL_WINDOWS):
        lo = jnp.clip(t - w // 2, 0, s)
        hi = jnp.clip(t + w // 2, 0, s)
        sl = slice(g * HEAD_DIM, (g + 1) * HEAD_DIM)
        cg = cs[:, :, sl]
        count = (hi - lo).astype(jnp.float32)[None, :, None]
        mean = (jnp.take(cg, hi, axis=1) - jnp.take(cg, lo, axis=1)) / count
        outs.append(mean - z32[:, :, sl])
    p = jnp.stack(outs, axis=2).astype(z.dtype)
    y = jnp.einsum("bsgc,gcd->bsgd", p, w_pool).reshape(b, s, D_GROUP)
    return y * scale


def hierarchical_moe(h, w_group, b_group, w_router, b_router, w_gate, w_up, w_down):
    t, d = h.shape
    h32 = h.astype(jnp.float32)
    lg = h32 @ w_group.astype(jnp.float32) + b_group.astype(jnp.float32)
    pg = jax.nn.softmax(lg, axis=-1)
    grp = jnp.argmax(lg, axis=-1).astype(jnp.int32)
    p_grp = jnp.take_along_axis(pg, grp[:, None], axis=-1)
    le = (h32 @ w_router.astype(jnp.float32) + b_router.astype(jnp.float32))
    le = le.reshape(t, N_EXPERT_GROUPS, EXPERTS_PER_GROUP)
    le = jnp.take_along_axis(le, grp[:, None, None], axis=1)[:, 0]
    pe = jax.nn.softmax(le, axis=-1)
    top_p, top_i = lax.top_k(pe, TOP_K)
    gate = p_grp * top_p / jnp.sum(top_p, axis=-1, keepdims=True)
    expert = grp[:, None] * EXPERTS_PER_GROUP + top_i.astype(jnp.int32)

    n_assign = t * TOP_K
    flat_e = expert.reshape(-1)
    flat_tok = jnp.repeat(jnp.arange(t, dtype=jnp.int32), TOP_K)
    flat_gate = gate.reshape(-1)
    order = jnp.argsort(flat_e)
    sorted_e = flat_e[order]
    counts = jnp.bincount(flat_e, length=N_EXPERTS)
    padded = (counts + MOE_BLOCK - 1) // MOE_BLOCK * MOE_BLOCK
    pad_end = jnp.cumsum(padded)
    pad_start = pad_end - padded
    start = jnp.cumsum(counts) - counts
    dest = pad_start[sorted_e] + jnp.arange(n_assign) - start[sorted_e]
    n_rows = ((n_assign + MOE_BLOCK - 1) // MOE_BLOCK + N_EXPERTS) * MOE_BLOCK
    n_blocks = n_rows // MOE_BLOCK
    row_tok = jnp.full((n_rows,), t, jnp.int32).at[dest].set(flat_tok[order])
    row_gate = jnp.zeros((n_rows,), h.dtype).at[dest].set(flat_gate[order].astype(h.dtype))
    blk_exp = jnp.minimum(
        jnp.searchsorted(pad_end, jnp.arange(n_blocks) * MOE_BLOCK, side="right"),
        N_EXPERTS - 1)
    h_pad = jnp.concatenate([h, jnp.zeros((1, d), h.dtype)], axis=0)
    xb = h_pad[row_tok].reshape(n_blocks, MOE_BLOCK, d)

    def expert_block(args):
        xblk, e = args
        a = jax.nn.silu(xblk @ w_gate[e]) * (xblk @ w_up[e])
        return a @ w_down[e]

    yb = lax.map(expert_block, (xb, blk_exp)).reshape(n_rows, d)
    return jax.ops.segment_sum(yb * row_gate[:, None], row_tok, num_segments=t + 1)[:t]


def setup_inputs(seed: int = 0) -> dict:
    key = jax.random.key(seed)
    ks = jax.random.split(key, 20)
    f32 = jnp.float32
    L = DEPTH

    def nrm(k, shape, scale):
        return jax.random.normal(k, shape, f32) * scale

    return {
        "x": nrm(ks[0], (BATCH, SEQ, D_MODEL), 1.0),
        "mix_norm": 1.0 + nrm(ks[1], (L, D_MODEL), 0.02),
        "w_in": nrm(ks[2], (L, D_MODEL, D_IN_PROJ), D_MODEL ** -0.5),
        "w_fourier": nrm(ks[3], (L, D_GROUP, D_GROUP), D_GROUP ** -0.5),
        "w_spatial": nrm(ks[4], (L, N_HEADS, CHUNK, CHUNK), CHUNK ** -0.5),
        "b_spatial": 1.0 + nrm(ks[5], (L, N_HEADS, CHUNK), 0.1),
        "conv_w": nrm(ks[6], (L, CONV_WIDTH, D_GROUP), CONV_WIDTH ** -0.5),
        "w_pool": nrm(ks[7], (L, N_POOL, HEAD_DIM, HEAD_DIM), HEAD_DIM ** -0.5),
        "pool_scale": 1.0 + nrm(ks[8], (L, D_GROUP), 0.1),
        "out_norm": 1.0 + nrm(ks[9], (L, N_MIXERS, D_GROUP), 0.02),
        "w_out": nrm(ks[10], (L, D_MIX, D_MODEL), D_MIX ** -0.5),
        "ffn_norm": 1.0 + nrm(ks[11], (L, D_MODEL), 0.02),
        "w_group": nrm(ks[12], (L, D_MODEL, N_EXPERT_GROUPS), D_MODEL ** -0.5),
        "b_group": nrm(ks[13], (L, N_EXPERT_GROUPS), 0.01),
        "w_router": nrm(ks[14], (L, D_MODEL, N_EXPERTS), D_MODEL ** -0.5),
        "b_router": nrm(ks[15], (L, N_EXPERTS), 0.01),
        "w_gate": nrm(ks[16], (L, N_EXPERTS, D_MODEL, D_EXPERT), D_MODEL ** -0.5),
        "w_up": nrm(ks[17], (L, N_EXPERTS, D_MODEL, D_EXPERT), D_MODEL ** -0.5),
        "w_down": nrm(ks[18], (L, N_EXPERTS, D_EXPERT, D_MODEL), D_EXPERT ** -0.5),
        "final_norm": 1.0 + nrm(ks[19], (D_MODEL,), 0.02),
    }


def reference(x, mix_norm, w_in, w_fourier, w_spatial, b_spatial, conv_w, w_pool,
              pool_scale, out_norm, w_out, ffn_norm, w_group, b_group, w_router,
              b_router, w_gate, w_up, w_down, final_norm):
    b, s, d = x.shape
    for l in range(DEPTH):
        h = rmsnorm(x, mix_norm[l])
        z = h @ w_in[l]
        z_fourier = z[..., :D_GROUP]
        z_sgu = z[..., D_GROUP:3 * D_GROUP]
        z_conv = z[..., 3 * D_GROUP:6 * D_GROUP]
        z_pool = z[..., 6 * D_GROUP:]
        y = jnp.concatenate([
            fourier_mixer(z_fourier, w_fourier[l]),
            spatial_gating_mixer(z_sgu, w_spatial[l], b_spatial[l]),
            short_conv_mixer(z_conv, conv_w[l]),
            multiscale_pool_mixer(z_pool, w_pool[l], pool_scale[l]),
        ], axis=-1).reshape(b, s, N_MIXERS, D_GROUP)
        y = rmsnorm(y, out_norm[l]).reshape(b, s, D_MIX)
        x = x + y @ w_out[l]
        h = rmsnorm(x, ffn_norm[l]).reshape(b * s, d)
        x = x + hierarchical_moe(h, w_group[l], b_group[l], w_router[l], b_router[l],
                                 w_gate[l], w_up[l], w_down[l]).reshape(b, s, d)
    return rmsnorm(x, final_norm)
```

```python
import functools

import numpy as np
import jax
import jax.numpy as jnp
from jax import lax
from jax.experimental import pallas as pl
from jax.experimental.pallas import tpu as pltpu

D_MODEL = 2048
BATCH = 4
SEQ = 4096
DEPTH = 4
N_TOK = BATCH * SEQ
D_GROUP = 512
HEAD_DIM = 128
N_HEADS = 4
CHUNK = 128
POOL_WINDOWS = (2, 4, 8, 16)
D_IN_PROJ = 7 * D_GROUP
N_EXPERT_GROUPS = 4
EXPERTS_PER_GROUP = 8
N_EXPERTS = 32
D_EXPERT = 768
EPS = 1e-6

F32 = jnp.float32
BF16 = jnp.bfloat16

FFT_Q = 8
FFT_P = SEQ // FFT_Q
FFT_HEADS = 2
FFT_W = FFT_HEADS * HEAD_DIM
HALO = 8
LANES = 128
ROUTE_OFF = N_EXPERT_GROUPS

TM_MOE = 256
N_BLOCKS = (2 * N_TOK) // TM_MOE + N_EXPERTS
N_ROWS = N_BLOCKS * TM_MOE
D_HALF = D_MODEL // 2

VMEM_LIMIT = 56 * 1024 * 1024


def _params(sem, vmem=VMEM_LIMIT):
    return pltpu.CompilerParams(dimension_semantics=sem, vmem_limit_bytes=vmem)


def _inproj_body(x_ref, g_ref, w_ref, z_ref, h_scr):
    @pl.when(pl.program_id(1) == 0)
    def _():
        x = x_ref[...]
        ms = jnp.mean(x * x, axis=-1, keepdims=True)
        h_scr[...] = ((x * lax.rsqrt(ms + EPS)) * g_ref[...]).astype(BF16)

    z_ref[...] = jnp.dot(h_scr[...], w_ref[...], preferred_element_type=F32)


def _inproj(x, g, w):
    tm, tn = 512, 512
    return pl.pallas_call(
        _inproj_body,
        grid=(N_TOK // tm, D_IN_PROJ // tn),
        in_specs=[
            pl.BlockSpec((tm, D_MODEL), lambda i, j: (i, 0)),
            pl.BlockSpec((1, D_MODEL), lambda i, j: (0, 0)),
            pl.BlockSpec((D_MODEL, tn), lambda i, j: (0, j)),
        ],
        out_specs=pl.BlockSpec((tm, tn), lambda i, j: (i, j)),
        out_shape=jax.ShapeDtypeStruct((N_TOK, D_IN_PROJ), F32),
        scratch_shapes=[pltpu.VMEM((tm, D_MODEL), BF16)],
        compiler_params=_params(("parallel", "arbitrary")),
        name="inproj",
    )(x, g, w)


def _fourier_consts():
    c = np.arange(HEAD_DIM)
    ang = 2.0 * np.pi * (np.outer(c, c) % HEAD_DIM) / HEAD_DIM
    cc, sc = np.cos(ang), np.sin(ang)
    scale = 1.0 / np.sqrt(SEQ * HEAD_DIM)
    cs = np.block([[cc, -sc], [sc, cc]]) * scale
    s1 = np.arange(FFT_P)
    m2 = np.zeros((FFT_Q, FFT_P, 2 * FFT_P), np.float64)
    for k2 in range(FFT_Q):
        k = FFT_Q * np.arange(FFT_P) + k2
        th = 2.0 * np.pi * (np.outer(k, s1) % SEQ) / SEQ
        m2[k2, :, :FFT_P] = np.cos(th)
        m2[k2, :, FFT_P:] = np.sin(th)
    return jnp.asarray(cs, BF16), jnp.asarray(m2, BF16)


def _fourier_body(z_ref, cs_ref, m2_ref, f_ref, y_scr, u_scr):
    rc = 64
    r = np.float32(np.sqrt(0.5))

    def chunk(ci, carry):
        r0 = pl.multiple_of(ci * rc, rc)
        zb = [z_ref[pl.ds(s2 * FFT_P + r0, rc), :] for s2 in range(FFT_Q)]
        e0, e1 = zb[0] + zb[4], zb[0] - zb[4]
        e2, e3 = zb[2] + zb[6], zb[2] - zb[6]
        o0, o1 = zb[1] + zb[5], zb[1] - zb[5]
        o2, o3 = zb[3] + zb[7], zb[3] - zb[7]
        p, q = (o1 - o3) * r, (o1 + o3) * r
        ee, oo = e0 + e2, o0 + o2
        ed, od = e0 - e2, o0 - o2
        zero = jnp.zeros_like(e0)
        ys = [(ee + oo, zero), (e1 + p, -e3 - q), (ed, -od), (e1 - p, e3 - q),
              (ee - oo, zero), (e1 - p, q - e3), (ed, od), (e1 + p, e3 + q)]
        for k2 in range(FFT_Q):
            re, im = ys[k2]
            rows = pl.ds(k2 * FFT_P + r0, rc)
            for h in range(FFT_HEADS):
                hs = slice(h * HEAD_DIM, (h + 1) * HEAD_DIM)
                y_scr[rows, 2 * h * HEAD_DIM:(2 * h + 1) * HEAD_DIM] = re[:, hs].astype(BF16)
                y_scr[rows, (2 * h + 1) * HEAD_DIM:(2 * h + 2) * HEAD_DIM] = im[:, hs].astype(BF16)
        return carry

    lax.fori_loop(0, FFT_P // rc, chunk, 0)

    for k2 in range(FFT_Q):
        for h in range(FFT_HEADS):
            yk = y_scr[k2 * FFT_P:(k2 + 1) * FFT_P, 2 * h * HEAD_DIM:(2 * h + 2) * HEAD_DIM]
            ab = jnp.dot(yk, cs_ref[...], preferred_element_type=F32)
            hs = slice(h * HEAD_DIM, (h + 1) * HEAD_DIM)
            u_scr[k2, 0:FFT_P, hs] = ab[:, :HEAD_DIM].astype(BF16)
            u_scr[k2, FFT_P:2 * FFT_P, hs] = ab[:, HEAD_DIM:].astype(BF16)

    for k2 in range(FFT_Q):
        res = jnp.dot(m2_ref[k2], u_scr[k2], preferred_element_type=F32)
        for h in range(FFT_HEADS):
            f_ref[h, pl.ds(k2, FFT_P, stride=FFT_Q), :] = res[:, h * HEAD_DIM:(h + 1) * HEAD_DIM]


def _fourier(z3, cs, m2):
    return pl.pallas_call(
        _fourier_body,
        grid=(BATCH, N_HEADS // FFT_HEADS),
        in_specs=[
            pl.BlockSpec((None, SEQ, FFT_W), lambda b, h: (b, 0, h)),
            pl.BlockSpec((2 * HEAD_DIM, 2 * HEAD_DIM), lambda b, h: (0, 0)),
            pl.BlockSpec((FFT_Q, FFT_P, 2 * FFT_P), lambda b, h: (0, 0, 0)),
        ],
        out_specs=pl.BlockSpec((None, FFT_HEADS, SEQ, HEAD_DIM), lambda b, h: (b, h, 0, 0)),
        out_shape=jax.ShapeDtypeStruct((BATCH, N_HEADS, SEQ, HEAD_DIM), F32),
        scratch_shapes=[
            pltpu.VMEM((SEQ, 2 * FFT_W), BF16),
            pltpu.VMEM((FFT_Q, 2 * FFT_P, FFT_W), BF16),
        ],
        compiler_params=_params(("parallel", "parallel")),
        name="fourier",
    )(z3, cs, m2)


def _gelu(x):
    return 0.5 * x * (1.0 + lax.erf(x * np.float32(np.sqrt(0.5))))


def _mixer_body(x_ref, f_ref, zu_ref, zv_ref, zb_ref, zc_ref, zval_ref, zp_ref,
                zc_prev, zval_prev, zp_prev, zc_next, zval_next, zp_next,
                wf_ref, ws_ref, bs_ref, cw_ref, wp_ref, ps_ref, on_ref, wo_ref,
                o_ref, ext_a, ext_p, ybf):
    i = pl.program_id(1)
    ts = x_ref.shape[0]
    keep_prev = (i > 0).astype(F32)
    keep_next = (i < pl.num_programs(1) - 1).astype(F32)

    def norm_store(y, g):
        gs = slice(g * D_GROUP, (g + 1) * D_GROUP)
        ms = jnp.mean(y * y, axis=-1, keepdims=True)
        ybf[:, gs] = ((y * lax.rsqrt(ms + EPS)) * on_ref[:, gs]).astype(BF16)

    spec = jnp.concatenate([f_ref[h] for h in range(N_HEADS)], axis=1).astype(BF16)
    norm_store(jnp.dot(spec, wf_ref[...], preferred_element_type=F32), 0)

    gu = _gelu(zu_ref[...])
    gv = _gelu(zv_ref[...]).astype(BF16)
    cols = []
    for h in range(N_HEADS):
        hs = slice(h * HEAD_DIM, (h + 1) * HEAD_DIM)
        rows = []
        for n in range(ts // CHUNK):
            vv = gv[n * CHUNK:(n + 1) * CHUNK, hs]
            rows.append(jnp.dot(ws_ref[h], vv, preferred_element_type=F32) + bs_ref[h])
        cols.append(jnp.concatenate(rows, axis=0))
    norm_store(gu * jnp.concatenate(cols, axis=1), 1)

    a = zc_ref[...] * zval_ref[...]
    ext_a[0:HALO, :] = zc_prev[...] * zval_prev[...] * keep_prev
    ext_a[HALO:HALO + ts, :] = a
    ext_a[HALO + ts:2 * HALO + ts, :] = zc_next[...] * zval_next[...] * keep_next
    conv = (cw_ref[0:1, :] * ext_a[HALO - 1:HALO - 1 + ts, :] + cw_ref[1:2, :] * a
            + cw_ref[2:3, :] * ext_a[HALO + 1:HALO + 1 + ts, :])
    norm_store(zb_ref[...] * conv, 2)

    zp = zp_ref[...]
    ext_p[0:HALO, :] = zp_prev[...] * keep_prev
    ext_p[HALO:HALO + ts, :] = zp
    ext_p[HALO + ts:2 * HALO + ts, :] = zp_next[...] * keep_next
    t = i * ts + lax.broadcasted_iota(jnp.int32, (ts, HEAD_DIM), 0)
    outs = []
    for g, w in enumerate(POOL_WINDOWS):
        gs = slice(g * HEAD_DIM, (g + 1) * HEAD_DIM)
        acc = ext_p[HALO - w // 2:HALO - w // 2 + ts, gs]
        for d in range(-w // 2 + 1, w // 2):
            acc = acc + ext_p[HALO + d:HALO + d + ts, gs]
        cnt = (jnp.minimum(t + w // 2, SEQ) - jnp.maximum(t - w // 2, 0)).astype(F32)
        pg = acc / cnt - zp[:, gs]
        outs.append(jnp.dot(pg.astype(BF16), wp_ref[g], preferred_element_type=F32))
    norm_store(jnp.concatenate(outs, axis=1) * ps_ref[...], 3)

    o_ref[...] = x_ref[...] + jnp.dot(ybf[...], wo_ref[...], preferred_element_type=F32)


def _mixer(x3, f3, z3, wf, ws, bsb, cw, wp, ps, on, wo):
    ts = 256
    nb8 = ts // HALO
    last8 = SEQ // HALO - 1

    def col(j):
        return pl.BlockSpec((None, ts, D_GROUP), lambda b, i, j=j: (b, i, j))

    def prev(j):
        return pl.BlockSpec((None, HALO, D_GROUP),
                            lambda b, i, j=j: (b, jnp.maximum(i * nb8 - 1, 0), j))

    def nxt(j):
        return pl.BlockSpec((None, HALO, D_GROUP),
                            lambda b, i, j=j: (b, jnp.minimum((i + 1) * nb8, last8), j))

    def full(shape):
        return pl.BlockSpec(shape, lambda b, i: (0,) * len(shape))

    return pl.pallas_call(
        _mixer_body,
        grid=(BATCH, SEQ // ts),
        in_specs=[
            pl.BlockSpec((None, ts, D_MODEL), lambda b, i: (b, i, 0)),
            pl.BlockSpec((None, N_HEADS, ts, HEAD_DIM), lambda b, i: (b, 0, i, 0)),
            col(1), col(2), col(3), col(4), col(5), col(6),
            prev(4), prev(5), prev(6), nxt(4), nxt(5), nxt(6),
            full((D_GROUP, D_GROUP)), full((N_HEADS, CHUNK, CHUNK)),
            full((N_HEADS, CHUNK, HEAD_DIM)), full((3, D_GROUP)),
            full((4, HEAD_DIM, HEAD_DIM)), full((1, D_GROUP)), full((1, D_MODEL)),
            full((D_MODEL, D_MODEL)),
        ],
        out_specs=pl.BlockSpec((None, ts, D_MODEL), lambda b, i: (b, i, 0)),
        out_shape=jax.ShapeDtypeStruct((BATCH, SEQ, D_MODEL), F32),
        scratch_shapes=[
            pltpu.VMEM((ts + 2 * HALO, D_GROUP), F32),
            pltpu.VMEM((ts + 2 * HALO, D_GROUP), F32),
            pltpu.VMEM((ts, D_MODEL), BF16),
        ],
        compiler_params=_params(("parallel", "arbitrary")),
        name="mixer",
    )(x3, f3, z3, z3, z3, z3, z3, z3, z3, z3, z3, z3, z3, z3, wf, ws, bsb, cw, wp, ps, on, wo)


def _router_body(x_ref, g_ref, wr_ref, br_ref, tri_ref, hp_ref, ri_ref, rg_ref, cnt_ref, carry):
    @pl.when(pl.program_id(0) == 0)
    def _():
        carry[...] = jnp.zeros_like(carry)

    x = x_ref[...]
    tr = x.shape[0]
    ms = jnp.mean(x * x, axis=-1, keepdims=True)
    h = (x * lax.rsqrt(ms + EPS)) * g_ref[...]

    bits = pltpu.bitcast(h.astype(BF16).astype(F32), jnp.uint32)
    hp_ref[...] = (bits[:, :D_HALF] >> 16) | (bits[:, D_HALF:] & jnp.uint32(0xFFFF0000))

    logits = jnp.dot(h, wr_ref[...], precision=lax.Precision.HIGHEST,
                     preferred_element_type=F32) + br_ref[...]
    lane = lax.broadcasted_iota(jnp.int32, (tr, LANES), 1)
    neg = np.float32(-np.inf)

    def first_argmax(v, vmax):
        return jnp.min(jnp.where(v == vmax, lane, LANES), axis=-1, keepdims=True)

    mg = jnp.where(lane < N_EXPERT_GROUPS, logits, neg)
    gmax = jnp.max(mg, axis=-1, keepdims=True)
    grp = first_argmax(mg, gmax)
    p_grp = 1.0 / jnp.sum(jnp.exp(mg - gmax), axis=-1, keepdims=True)

    lo = ROUTE_OFF + EXPERTS_PER_GROUP * grp
    le = jnp.where(lane >= lo, jnp.where(lane < lo + EXPERTS_PER_GROUP, logits, neg), neg)
    emax = jnp.max(le, axis=-1, keepdims=True)
    i1 = first_argmax(le, emax)
    le2 = jnp.where(lane == i1, neg, le)
    emax2 = jnp.max(le2, axis=-1, keepdims=True)
    i2 = first_argmax(le2, emax2)
    e2 = jnp.exp(emax2 - emax)
    gate1 = p_grp / (1.0 + e2)
    gate2 = p_grp * e2 / (1.0 + e2)

    is1, is2 = lane == i1, lane == i2
    oh = jnp.where(is1, 1.0, jnp.where(is2, 1.0, 0.0))
    pref = jnp.dot(tri_ref[...], oh.astype(BF16), preferred_element_type=F32) + carry[...]
    r1 = jnp.sum(jnp.where(is1, pref, 0.0), axis=-1, keepdims=True).astype(jnp.int32)
    r2 = jnp.sum(jnp.where(is2, pref, 0.0), axis=-1, keepdims=True).astype(jnp.int32)
    carry[...] = carry[...] + jnp.sum(oh, axis=0, keepdims=True)
    cnt_ref[...] = carry[...]

    ri_ref[...] = jnp.where(lane == 0, i1 - ROUTE_OFF,
                            jnp.where(lane == 1, i2 - ROUTE_OFF,
                                      jnp.where(lane == 2, r1, jnp.where(lane == 3, r2, 0))))
    rg_ref[...] = jnp.where(lane == 0, gate1, jnp.where(lane == 1, gate2, 0.0))


def _router(x, g, wr, br, tri):
    tr = tri.shape[0]
    return pl.pallas_call(
        _router_body,
        grid=(N_TOK // tr,),
        in_specs=[
            pl.BlockSpec((tr, D_MODEL), lambda i: (i, 0)),
            pl.BlockSpec((1, D_MODEL), lambda i: (0, 0)),
            pl.BlockSpec((D_MODEL, LANES), lambda i: (0, 0)),
            pl.BlockSpec((1, LANES), lambda i: (0, 0)),
            pl.BlockSpec((tr, tr), lambda i: (0, 0)),
        ],
        out_specs=[
            pl.BlockSpec((tr, D_HALF), lambda i: (i, 0)),
            pl.BlockSpec((tr, LANES), lambda i: (i, 0)),
            pl.BlockSpec((tr, LANES), lambda i: (i, 0)),
            pl.BlockSpec((1, LANES), lambda i: (0, 0)),
        ],
        out_shape=[
            jax.ShapeDtypeStruct((N_TOK, D_HALF), jnp.uint32),
            jax.ShapeDtypeStruct((N_TOK, LANES), jnp.int32),
            jax.ShapeDtypeStruct((N_TOK, LANES), F32),
            jax.ShapeDtypeStruct((1, LANES), F32),
        ],
        scratch_shapes=[pltpu.VMEM((1, LANES), F32)],
        compiler_params=_params(("arbitrary",)),
        name="router",
    )(x, g, wr, br, tri)


def _dispatch_body(dest_ref, hp_ref, xb_in, xb_out, sem):
    del xb_in
    td = hp_ref.shape[0]
    base = pl.program_id(0) * td

    def row_copy(r, k):
        d = dest_ref[2 * (base + r) + k]
        return pltpu.make_async_copy(hp_ref.at[pl.ds(r, 1), :], xb_out.at[pl.ds(d, 1), :], sem)

    def issue(r, c):
        row_copy(r, 0).start()
        row_copy(r, 1).start()
        return c

    def drain(r, c):
        row_copy(r, 0).wait()
        row_copy(r, 1).wait()
        return c

    lax.fori_loop(0, td, issue, 0)
    lax.fori_loop(0, td, drain, 0)


def _dispatch(dest, hp, xb0):
    td = 256
    return pl.pallas_call(
        _dispatch_body,
        grid_spec=pltpu.PrefetchScalarGridSpec(
            num_scalar_prefetch=1,
            grid=(N_TOK // td,),
            in_specs=[
                pl.BlockSpec((td, D_HALF), lambda i, d: (i, 0)),
                pl.BlockSpec(memory_space=pl.ANY),
            ],
            out_specs=pl.BlockSpec(memory_space=pl.ANY),
            scratch_shapes=[pltpu.SemaphoreType.DMA(())],
        ),
        out_shape=jax.ShapeDtypeStruct((N_ROWS, D_HALF), jnp.uint32),
        input_output_aliases={2: 0},
        compiler_params=_params(("arbitrary",)),
        name="dispatch",
    )(dest, hp, xb0)


def _moe_body(be_ref, na_ref, x_ref, wg_ref, wu_ref, wd_ref, y_ref):
    del be_ref

    @pl.when(pl.program_id(0) >= na_ref[0])
    def _():
        y_ref[...] = jnp.zeros_like(y_ref)

    @pl.when(pl.program_id(0) < na_ref[0])
    def _():
        xu = x_ref[...]
        lo = pltpu.bitcast(xu << 16, F32).astype(BF16)
        hi = pltpu.bitcast(xu & jnp.uint32(0xFFFF0000), F32).astype(BF16)
        fc = 256
        for c in range(D_EXPERT // fc):
            cs = slice(c * fc, (c + 1) * fc)
            g = (jnp.dot(lo, wg_ref[0:D_HALF, cs], preferred_element_type=F32)
                 + jnp.dot(hi, wg_ref[D_HALF:D_MODEL, cs], preferred_element_type=F32))
            u = (jnp.dot(lo, wu_ref[0:D_HALF, cs], preferred_element_type=F32)
                 + jnp.dot(hi, wu_ref[D_HALF:D_MODEL, cs], preferred_element_type=F32))
            a = (g * jax.nn.sigmoid(g) * u).astype(BF16)
            part = jnp.dot(a, wd_ref[cs, :], preferred_element_type=F32)
            if c == 0:
                y_ref[...] = part
            else:
                y_ref[...] += part


def _moe(blk_exp, n_active, xb, wg, wu, wd):
    def row_blk(i, be, na):
        return (jnp.minimum(i, na[0] - 1), 0)

    return pl.pallas_call(
        _moe_body,
        grid_spec=pltpu.PrefetchScalarGridSpec(
            num_scalar_prefetch=2,
            grid=(N_BLOCKS,),
            in_specs=[
                pl.BlockSpec((TM_MOE, D_HALF), row_blk),
                pl.BlockSpec((None, D_MODEL, D_EXPERT), lambda i, be, na: (be[i], 0, 0)),
                pl.BlockSpec((None, D_MODEL, D_EXPERT), lambda i, be, na: (be[i], 0, 0)),
                pl.BlockSpec((None, D_EXPERT, D_MODEL), lambda i, be, na: (be[i], 0, 0)),
            ],
            out_specs=pl.BlockSpec((TM_MOE, D_MODEL), lambda i, be, na: (i, 0)),
        ),
        out_shape=jax.ShapeDtypeStruct((N_ROWS, D_MODEL), F32),
        compiler_params=_params(("arbitrary",)),
        name="moe",
    )(blk_exp, n_active, xb, wg, wu, wd)


def _combine_body(dest_ref, x_ref, rg_ref, gfin_ref, yb_ref, o_ref, ybuf, sem, *, final):
    tc = x_ref.shape[0]
    base = pl.program_id(0) * tc

    def row_copy(r, k):
        d = dest_ref[2 * (base + r) + k]
        return pltpu.make_async_copy(yb_ref.at[pl.ds(d, 1), :], ybuf.at[k, pl.ds(r, 1), :], sem)

    def issue(r, c):
        row_copy(r, 0).start()
        row_copy(r, 1).start()
        return c

    def drain(r, c):
        row_copy(r, 0).wait()
        row_copy(r, 1).wait()
        return c

    lax.fori_loop(0, tc, issue, 0)
    lax.fori_loop(0, tc, drain, 0)

    rg = rg_ref[...]
    y = x_ref[...] + rg[:, 0:1] * ybuf[0] + rg[:, 1:2] * ybuf[1]
    if final:
        ms = jnp.mean(y * y, axis=-1, keepdims=True)
        y = (y * lax.rsqrt(ms + EPS)) * gfin_ref[...]
    o_ref[...] = y


def _combine(dest, x, rg, gfin, yb, final):
    tc = 256
    return pl.pallas_call(
        functools.partial(_combine_body, final=final),
        grid_spec=pltpu.PrefetchScalarGridSpec(
            num_scalar_prefetch=1,
            grid=(N_TOK // tc,),
            in_specs=[
                pl.BlockSpec((tc, D_MODEL), lambda i, d: (i, 0)),
                pl.BlockSpec((tc, LANES), lambda i, d: (i, 0)),
                pl.BlockSpec((1, D_MODEL), lambda i, d: (0, 0)),
                pl.BlockSpec(memory_space=pl.ANY),
            ],
            out_specs=pl.BlockSpec((tc, D_MODEL), lambda i, d: (i, 0)),
            scratch_shapes=[pltpu.VMEM((2, tc, D_MODEL), F32), pltpu.SemaphoreType.DMA(())],
        ),
        out_shape=jax.ShapeDtypeStruct((N_TOK, D_MODEL), F32),
        compiler_params=_params(("arbitrary",)),
        name="combine",
    )(dest, x, rg, gfin, yb)


def _plan(ri, cnt):
    counts = cnt[0, ROUTE_OFF:ROUTE_OFF + N_EXPERTS].astype(jnp.int32)
    nblk = (counts + TM_MOE - 1) // TM_MOE
    blk_end = jnp.cumsum(nblk)
    pad_start = (blk_end - nblk) * TM_MOE
    n_active = blk_end[-1:]
    blk = jnp.minimum(jnp.arange(N_BLOCKS, dtype=jnp.int32), n_active[0] - 1)
    blk_exp = jnp.minimum(jnp.searchsorted(blk_end, blk, side="right"), N_EXPERTS - 1)
    eid, rank = ri[:, 0:2], ri[:, 2:4]
    onehot = eid[:, :, None] == jnp.arange(N_EXPERTS, dtype=jnp.int32)[None, None, :]
    dest = rank + jnp.sum(jnp.where(onehot, pad_start[None, None, :], 0), axis=-1)
    return dest.reshape(-1).astype(jnp.int32), blk_exp.astype(jnp.int32), n_active.astype(jnp.int32)


def kernel(x, mix_norm, w_in, w_fourier, w_spatial, b_spatial, conv_w, w_pool, pool_scale,
           out_norm, w_out, ffn_norm, w_group, b_group, w_router, b_router, w_gate, w_up,
           w_down, final_norm):
    cs, m2 = _fourier_consts()
    tr = 512
    tri = (jnp.arange(tr)[:, None] > jnp.arange(tr)[None, :]).astype(BF16)

    w_in_b = w_in.astype(BF16)
    w_f_b = w_fourier.astype(BF16)
    w_s_b = w_spatial.astype(BF16)
    w_p_b = w_pool.astype(BF16)
    w_o_b = w_out.astype(BF16)
    w_g_b = w_gate.astype(BF16)
    w_u_b = w_up.astype(BF16)
    w_d_b = w_down.astype(BF16)
    pad = LANES - N_EXPERT_GROUPS - N_EXPERTS
    wr = jnp.concatenate([w_group, w_router, jnp.zeros((DEPTH, D_MODEL, pad), F32)], axis=-1)
    br = jnp.concatenate([b_group, b_router, jnp.zeros((DEPTH, pad), F32)], axis=-1)
    bsb = jnp.broadcast_to(b_spatial[:, :, :, None], (DEPTH, N_HEADS, CHUNK, HEAD_DIM))
    xb0 = jnp.zeros((N_ROWS, D_HALF), jnp.uint32)

    xt = x.reshape(N_TOK, D_MODEL)
    for l in range(DEPTH):
        z = _inproj(xt, mix_norm[l][None, :], w_in_b[l])
        z3 = z.reshape(BATCH, SEQ, D_IN_PROJ)
        f3 = _fourier(z3, cs, m2)
        x3 = _mixer(xt.reshape(BATCH, SEQ, D_MODEL), f3, z3, w_f_b[l], w_s_b[l], bsb[l],
                    conv_w[l], w_p_b[l], pool_scale[l][None, :],
                    out_norm[l].reshape(1, D_MODEL), w_o_b[l])
        xt = x3.reshape(N_TOK, D_MODEL)
        hp, ri, rg, cnt = _router(xt, ffn_norm[l][None, :], wr[l], br[l][None, :], tri)
        dest, blk_exp, n_active = _plan(ri, cnt)
        xb = _dispatch(dest, hp, xb0)
        yb = _moe(blk_exp, n_active, xb, w_g_b[l], w_u_b[l], w_d_b[l])
        xt = _combine(dest, xt, rg, final_norm[None, :], yb, final=(l == DEPTH - 1))
    return xt.reshape(BATCH, SEQ, D_MODEL)
```

```python
import functools

import numpy as np
import jax
import jax.numpy as jnp
from jax import lax
from jax.experimental import pallas as pl
from jax.experimental.pallas import tpu as pltpu

D_MODEL = 2048
BATCH = 4
SEQ = 4096
DEPTH = 4
N_TOK = BATCH * SEQ
D_GROUP = 512
HEAD_DIM = 128
N_HEADS = 4
CHUNK = 128
POOL_WINDOWS = (2, 4, 8, 16)
D_IN_PROJ = 7 * D_GROUP
N_EXPERT_GROUPS = 4
EXPERTS_PER_GROUP = 8
N_EXPERTS = 32
D_EXPERT = 768
EPS = 1e-6

F32 = jnp.float32
BF16 = jnp.bfloat16

FFT_Q = 8
FFT_P = SEQ // FFT_Q
FFT_HEADS = 2
FFT_W = FFT_HEADS * HEAD_DIM
HALO = 8
LANES = 128
ROUTE_OFF = N_EXPERT_GROUPS

TM_MOE = 256
N_BLOCKS = (2 * N_TOK) // TM_MOE + N_EXPERTS
N_ROWS = N_BLOCKS * TM_MOE
D_HALF = D_MODEL // 2

VMEM_LIMIT = 56 * 1024 * 1024


def _params(sem, vmem=VMEM_LIMIT):
    return pltpu.CompilerParams(dimension_semantics=sem, vmem_limit_bytes=vmem)


def _inproj_body(x_ref, g_ref, w_ref, z_ref):
    x = x_ref[...]
    ms = jnp.mean(x * x, axis=-1, keepdims=True)
    h = ((x * lax.rsqrt(ms + EPS)) * g_ref[...]).astype(BF16)
    tn = 512
    for j in range(D_IN_PROJ // tn):
        cs = slice(j * tn, (j + 1) * tn)
        z_ref[:, cs] = jnp.dot(h, w_ref[:, cs], preferred_element_type=F32)


def _inproj(l, x, g, w):
    tm = 512
    return pl.pallas_call(
        _inproj_body,
        grid=(N_TOK // tm,),
        in_specs=[
            pl.BlockSpec((tm, D_MODEL), lambda i: (i, 0)),
            pl.BlockSpec((None, 1, D_MODEL), lambda i: (l, 0, 0)),
            pl.BlockSpec((None, D_MODEL, D_IN_PROJ), lambda i: (l, 0, 0),
                         pipeline_mode=pl.Buffered(1)),
        ],
        out_specs=pl.BlockSpec((tm, D_IN_PROJ), lambda i: (i, 0)),
        out_shape=jax.ShapeDtypeStruct((N_TOK, D_IN_PROJ), F32),
        compiler_params=_params(("parallel",)),
        name="inproj",
    )(x, g, w)


def _fourier_consts():
    c = np.arange(HEAD_DIM)
    ang = 2.0 * np.pi * (np.outer(c, c) % HEAD_DIM) / HEAD_DIM
    cc, sc = np.cos(ang), np.sin(ang)
    scale = 1.0 / np.sqrt(SEQ * HEAD_DIM)
    cs = np.block([[cc, -sc], [sc, cc]]) * scale
    s1 = np.arange(FFT_P)
    m2 = np.zeros((FFT_Q, FFT_P, 2 * FFT_P), np.float64)
    for k2 in range(FFT_Q):
        k = FFT_Q * np.arange(FFT_P) + k2
        th = 2.0 * np.pi * (np.outer(k, s1) % SEQ) / SEQ
        m2[k2, :, :FFT_P] = np.cos(th)
        m2[k2, :, FFT_P:] = np.sin(th)
    return jnp.asarray(cs, BF16), jnp.asarray(m2, BF16)


def _fourier_body(z_ref, cs_ref, m2_ref, f_ref, y_scr, u_scr):
    rc = 64
    r = np.float32(np.sqrt(0.5))

    def chunk(ci, carry):
        r0 = pl.multiple_of(ci * rc, rc)
        zb = [z_ref[pl.ds(s2 * FFT_P + r0, rc), :] for s2 in range(FFT_Q)]
        e0, e1 = zb[0] + zb[4], zb[0] - zb[4]
        e2, e3 = zb[2] + zb[6], zb[2] - zb[6]
        o0, o1 = zb[1] + zb[5], zb[1] - zb[5]
        o2, o3 = zb[3] + zb[7], zb[3] - zb[7]
        p, q = (o1 - o3) * r, (o1 + o3) * r
        ee, oo = e0 + e2, o0 + o2
        ed, od = e0 - e2, o0 - o2
        zero = jnp.zeros_like(e0)
        ys = [(ee + oo, zero), (e1 + p, -e3 - q), (ed, -od), (e1 - p, e3 - q),
              (ee - oo, zero), (e1 - p, q - e3), (ed, od), (e1 + p, e3 + q)]
        for k2 in range(FFT_Q):
            re, im = ys[k2]
            rows = pl.ds(k2 * FFT_P + r0, rc)
            for h in range(FFT_HEADS):
                hs = slice(h * HEAD_DIM, (h + 1) * HEAD_DIM)
                y_scr[rows, 2 * h * HEAD_DIM:(2 * h + 1) * HEAD_DIM] = re[:, hs].astype(BF16)
                y_scr[rows, (2 * h + 1) * HEAD_DIM:(2 * h + 2) * HEAD_DIM] = im[:, hs].astype(BF16)
        return carry

    lax.fori_loop(0, FFT_P // rc, chunk, 0)

    for k2 in range(FFT_Q):
        for h in range(FFT_HEADS):
            yk = y_scr[k2 * FFT_P:(k2 + 1) * FFT_P, 2 * h * HEAD_DIM:(2 * h + 2) * HEAD_DIM]
            ab = jnp.dot(yk, cs_ref[...], preferred_element_type=F32)
            hs = slice(h * HEAD_DIM, (h + 1) * HEAD_DIM)
            u_scr[k2, 0:FFT_P, hs] = ab[:, :HEAD_DIM].astype(BF16)
            u_scr[k2, FFT_P:2 * FFT_P, hs] = ab[:, HEAD_DIM:].astype(BF16)

    for k2 in range(FFT_Q):
        res = jnp.dot(m2_ref[k2], u_scr[k2], preferred_element_type=F32)
        for h in range(FFT_HEADS):
            f_ref[h, pl.ds(k2, FFT_P, stride=FFT_Q), :] = res[:, h * HEAD_DIM:(h + 1) * HEAD_DIM]


def _fourier(z3, cs, m2):
    return pl.pallas_call(
        _fourier_body,
        grid=(BATCH, N_HEADS // FFT_HEADS),
        in_specs=[
            pl.BlockSpec((None, SEQ, FFT_W), lambda b, h: (b, 0, h)),
            pl.BlockSpec((2 * HEAD_DIM, 2 * HEAD_DIM), lambda b, h: (0, 0)),
            pl.BlockSpec((FFT_Q, FFT_P, 2 * FFT_P), lambda b, h: (0, 0, 0)),
        ],
        out_specs=pl.BlockSpec((None, FFT_HEADS, SEQ, HEAD_DIM), lambda b, h: (b, h, 0, 0)),
        out_shape=jax.ShapeDtypeStruct((BATCH, N_HEADS, SEQ, HEAD_DIM), F32),
        scratch_shapes=[
            pltpu.VMEM((SEQ, 2 * FFT_W), BF16),
            pltpu.VMEM((FFT_Q, 2 * FFT_P, FFT_W), BF16),
        ],
        compiler_params=_params(("parallel", "parallel")),
        name="fourier",
    )(z3, cs, m2)


def _gelu(x):
    return 0.5 * x * (1.0 + lax.erf(x * np.float32(np.sqrt(0.5))))


def _mixer_body(x_ref, f_ref, zu_ref, zv_ref, zb_ref, zc_ref, zval_ref, zp_ref,
                zc_prev, zval_prev, zp_prev, zc_next, zval_next, zp_next,
                wf_ref, ws_ref, bs_ref, cw_ref, wp_ref, ps_ref, on_ref, wo_ref,
                o_ref, ext_a, ext_p, ybf):
    i = pl.program_id(1)
    ts = x_ref.shape[0]
    keep_prev = (i > 0).astype(F32)
    keep_next = (i < pl.num_programs(1) - 1).astype(F32)

    def norm_store(y, g):
        gs = slice(g * D_GROUP, (g + 1) * D_GROUP)
        ms = jnp.mean(y * y, axis=-1, keepdims=True)
        ybf[:, gs] = ((y * lax.rsqrt(ms + EPS)) * on_ref[:, gs]).astype(BF16)

    spec = jnp.concatenate([f_ref[h] for h in range(N_HEADS)], axis=1).astype(BF16)
    norm_store(jnp.dot(spec, wf_ref[...], preferred_element_type=F32), 0)

    gu = _gelu(zu_ref[...])
    gv = _gelu(zv_ref[...]).astype(BF16)
    cols = []
    for h in range(N_HEADS):
        hs = slice(h * HEAD_DIM, (h + 1) * HEAD_DIM)
        rows = []
        for n in range(ts // CHUNK):
            vv = gv[n * CHUNK:(n + 1) * CHUNK, hs]
            rows.append(jnp.dot(ws_ref[h], vv, preferred_element_type=F32) + bs_ref[h])
        cols.append(jnp.concatenate(rows, axis=0))
    norm_store(gu * jnp.concatenate(cols, axis=1), 1)

    a = zc_ref[...] * zval_ref[...]
    ext_a[0:HALO, :] = zc_prev[...] * zval_prev[...] * keep_prev
    ext_a[HALO:HALO + ts, :] = a
    ext_a[HALO + ts:2 * HALO + ts, :] = zc_next[...] * zval_next[...] * keep_next
    conv = (cw_ref[0:1, :] * ext_a[HALO - 1:HALO - 1 + ts, :] + cw_ref[1:2, :] * a
            + cw_ref[2:3, :] * ext_a[HALO + 1:HALO + 1 + ts, :])
    norm_store(zb_ref[...] * conv, 2)

    zp = zp_ref[...]
    ext_p[0:HALO, :] = zp_prev[...] * keep_prev
    ext_p[HALO:HALO + ts, :] = zp
    ext_p[HALO + ts:2 * HALO + ts, :] = zp_next[...] * keep_next
    t = i * ts + lax.broadcasted_iota(jnp.int32, (ts, HEAD_DIM), 0)
    outs = []
    for g, w in enumerate(POOL_WINDOWS):
        gs = slice(g * HEAD_DIM, (g + 1) * HEAD_DIM)
        acc = ext_p[HALO - w // 2:HALO - w // 2 + ts, gs]
        for d in range(-w // 2 + 1, w // 2):
            acc = acc + ext_p[HALO + d:HALO + d + ts, gs]
        cnt = (jnp.minimum(t + w // 2, SEQ) - jnp.maximum(t - w // 2, 0)).astype(F32)
        pg = acc / cnt - zp[:, gs]
        outs.append(jnp.dot(pg.astype(BF16), wp_ref[g], preferred_element_type=F32))
    norm_store(jnp.concatenate(outs, axis=1) * ps_ref[...], 3)

    o_ref[...] = x_ref[...] + jnp.dot(ybf[...], wo_ref[...], preferred_element_type=F32)


def _mixer(l, x3, f3, z3, wf, ws, bsb, cw, wp, ps, on, wo):
    ts = 512
    nb8 = ts // HALO
    last8 = SEQ // HALO - 1

    def col(j):
        return pl.BlockSpec((None, ts, D_GROUP), lambda b, i, j=j: (b, i, j))

    def prev(j):
        return pl.BlockSpec((None, HALO, D_GROUP),
                            lambda b, i, j=j: (b, jnp.maximum(i * nb8 - 1, 0), j))

    def nxt(j):
        return pl.BlockSpec((None, HALO, D_GROUP),
                            lambda b, i, j=j: (b, jnp.minimum((i + 1) * nb8, last8), j))

    def full(shape, **kw):
        return pl.BlockSpec((None,) + shape, lambda b, i: (l,) + (0,) * len(shape), **kw)

    return pl.pallas_call(
        _mixer_body,
        grid=(BATCH, SEQ // ts),
        in_specs=[
            pl.BlockSpec((None, ts, D_MODEL), lambda b, i: (b, i, 0)),
            pl.BlockSpec((None, N_HEADS, ts, HEAD_DIM), lambda b, i: (b, 0, i, 0)),
            col(1), col(2), col(3), col(4), col(5), col(6),
            prev(4), prev(5), prev(6), nxt(4), nxt(5), nxt(6),
            full((D_GROUP, D_GROUP)), full((N_HEADS, CHUNK, CHUNK)),
            full((N_HEADS, CHUNK, HEAD_DIM)), full((3, D_GROUP)),
            full((4, HEAD_DIM, HEAD_DIM)), full((1, D_GROUP)), full((1, D_MODEL)),
            full((D_MODEL, D_MODEL), pipeline_mode=pl.Buffered(1)),
        ],
        out_specs=pl.BlockSpec((None, ts, D_MODEL), lambda b, i: (b, i, 0)),
        out_shape=jax.ShapeDtypeStruct((BATCH, SEQ, D_MODEL), F32),
        scratch_shapes=[
            pltpu.VMEM((ts + 2 * HALO, D_GROUP), F32),
            pltpu.VMEM((ts + 2 * HALO, D_GROUP), F32),
            pltpu.VMEM((ts, D_MODEL), BF16),
        ],
        compiler_params=_params(("parallel", "arbitrary")),
        name="mixer",
    )(x3, f3, z3, z3, z3, z3, z3, z3, z3, z3, z3, z3, z3, z3, wf, ws, bsb, cw, wp, ps, on, wo)


def _router_body(x_ref, g_ref, wr_ref, br_ref, tri_ref, hp_ref, ri_ref, rg_ref, cnt_ref, carry):
    @pl.when(pl.program_id(0) == 0)
    def _():
        carry[...] = jnp.zeros_like(carry)

    x = x_ref[...]
    tr = x.shape[0]
    ms = jnp.mean(x * x, axis=-1, keepdims=True)
    h = (x * lax.rsqrt(ms + EPS)) * g_ref[...]

    bits = pltpu.bitcast(h.astype(BF16).astype(F32), jnp.uint32)
    hp_ref[...] = (bits[:, :D_HALF] >> 16) | (bits[:, D_HALF:] & jnp.uint32(0xFFFF0000))

    logits = jnp.dot(h, wr_ref[...], precision=lax.Precision.HIGHEST,
                     preferred_element_type=F32) + br_ref[...]
    lane = lax.broadcasted_iota(jnp.int32, (tr, LANES), 1)
    neg = np.float32(-np.inf)

    def first_argmax(v, vmax):
        return jnp.min(jnp.where(v == vmax, lane, LANES), axis=-1, keepdims=True)

    mg = jnp.where(lane < N_EXPERT_GROUPS, logits, neg)
    gmax = jnp.max(mg, axis=-1, keepdims=True)
    grp = first_argmax(mg, gmax)
    p_grp = 1.0 / jnp.sum(jnp.exp(mg - gmax), axis=-1, keepdims=True)

    lo = ROUTE_OFF + EXPERTS_PER_GROUP * grp
    le = jnp.where(lane >= lo, jnp.where(lane < lo + EXPERTS_PER_GROUP, logits, neg), neg)
    emax = jnp.max(le, axis=-1, keepdims=True)
    i1 = first_argmax(le, emax)
    le2 = jnp.where(lane == i1, neg, le)
    emax2 = jnp.max(le2, axis=-1, keepdims=True)
    i2 = first_argmax(le2, emax2)
    e2 = jnp.exp(emax2 - emax)
    gate1 = p_grp / (1.0 + e2)
    gate2 = p_grp * e2 / (1.0 + e2)

    is1, is2 = lane == i1, lane == i2
    oh = jnp.where(is1, 1.0, jnp.where(is2, 1.0, 0.0))
    pref = jnp.dot(tri_ref[...], oh.astype(BF16), preferred_element_type=F32) + carry[...]
    r1 = jnp.sum(jnp.where(is1, pref, 0.0), axis=-1, keepdims=True).astype(jnp.int32)
    r2 = jnp.sum(jnp.where(is2, pref, 0.0), axis=-1, keepdims=True).astype(jnp.int32)
    carry[...] = carry[...] + jnp.sum(oh, axis=0, keepdims=True)
    cnt_ref[...] = carry[...]

    ri_ref[...] = jnp.where(lane == 0, i1 - ROUTE_OFF,
                            jnp.where(lane == 1, i2 - ROUTE_OFF,
                                      jnp.where(lane == 2, r1, jnp.where(lane == 3, r2, 0))))
    rg_ref[...] = jnp.where(lane == 0, gate1, jnp.where(lane == 1, gate2, 0.0))


def _router(l, x, g, wr, br, tri):
    tr = tri.shape[0]
    return pl.pallas_call(
        _router_body,
        grid=(N_TOK // tr,),
        in_specs=[
            pl.BlockSpec((tr, D_MODEL), lambda i: (i, 0)),
            pl.BlockSpec((None, 1, D_MODEL), lambda i: (l, 0, 0)),
            pl.BlockSpec((None, D_MODEL, LANES), lambda i: (l, 0, 0)),
            pl.BlockSpec((None, 1, LANES), lambda i: (l, 0, 0)),
            pl.BlockSpec((tr, tr), lambda i: (0, 0)),
        ],
        out_specs=[
            pl.BlockSpec((tr, D_HALF), lambda i: (i, 0)),
            pl.BlockSpec((tr, LANES), lambda i: (i, 0)),
            pl.BlockSpec((tr, LANES), lambda i: (i, 0)),
            pl.BlockSpec((1, LANES), lambda i: (0, 0)),
        ],
        out_shape=[
            jax.ShapeDtypeStruct((N_TOK, D_HALF), jnp.uint32),
            jax.ShapeDtypeStruct((N_TOK, LANES), jnp.int32),
            jax.ShapeDtypeStruct((N_TOK, LANES), F32),
            jax.ShapeDtypeStruct((1, LANES), F32),
        ],
        scratch_shapes=[pltpu.VMEM((1, LANES), F32)],
        compiler_params=_params(("arbitrary",)),
        name="router",
    )(x, g, wr, br, tri)


def _dispatch_body(dest_ref, hp_ref, xb_in, xb_out, sem):
    del xb_in
    td = hp_ref.shape[0]
    base = pl.program_id(0) * td

    def row_copy(r, k):
        d = dest_ref[2 * (base + r) + k]
        return pltpu.make_async_copy(hp_ref.at[pl.ds(r, 1), :], xb_out.at[pl.ds(d, 1), :], sem)

    def issue(r, c):
        row_copy(r, 0).start()
        row_copy(r, 1).start()
        return c

    def drain(r, c):
        row_copy(r, 0).wait()
        row_copy(r, 1).wait()
        return c

    lax.fori_loop(0, td, issue, 0)
    lax.fori_loop(0, td, drain, 0)


def _dispatch(dest, hp, xb0):
    td = 256
    return pl.pallas_call(
        _dispatch_body,
        grid_spec=pltpu.PrefetchScalarGridSpec(
            num_scalar_prefetch=1,
            grid=(N_TOK // td,),
            in_specs=[
                pl.BlockSpec((td, D_HALF), lambda i, d: (i, 0)),
                pl.BlockSpec(memory_space=pl.ANY),
            ],
            out_specs=pl.BlockSpec(memory_space=pl.ANY),
            scratch_shapes=[pltpu.SemaphoreType.DMA(())],
        ),
        out_shape=jax.ShapeDtypeStruct((N_ROWS, D_HALF), jnp.uint32),
        input_output_aliases={2: 0},
        compiler_params=_params(("arbitrary",)),
        name="dispatch",
    )(dest, hp, xb0)


def _moe_body(be_ref, na_ref, x_ref, wg32_ref, wu32_ref, wd32_ref, y_ref, wg_ref, wu_ref, wd_ref):
    i = pl.program_id(0)

    @pl.when(i >= na_ref[0])
    def _():
        y_ref[...] = jnp.zeros_like(y_ref)

    @pl.when((i == 0) | (be_ref[i] != be_ref[jnp.maximum(i - 1, 0)]))
    def _():
        wg_ref[...] = wg32_ref[...].astype(BF16)
        wu_ref[...] = wu32_ref[...].astype(BF16)
        wd_ref[...] = wd32_ref[...].astype(BF16)

    @pl.when(i < na_ref[0])
    def _():
        xu = x_ref[...]
        lo = pltpu.bitcast(xu << 16, F32).astype(BF16)
        hi = pltpu.bitcast(xu & jnp.uint32(0xFFFF0000), F32).astype(BF16)
        fc = 256
        for c in range(D_EXPERT // fc):
            cs = slice(c * fc, (c + 1) * fc)
            g = (jnp.dot(lo, wg_ref[0:D_HALF, cs], preferred_element_type=F32)
                 + jnp.dot(hi, wg_ref[D_HALF:D_MODEL, cs], preferred_element_type=F32))
            u = (jnp.dot(lo, wu_ref[0:D_HALF, cs], preferred_element_type=F32)
                 + jnp.dot(hi, wu_ref[D_HALF:D_MODEL, cs], preferred_element_type=F32))
            a = (g * jax.nn.sigmoid(g) * u).astype(BF16)
            part = jnp.dot(a, wd_ref[cs, :], preferred_element_type=F32)
            if c == 0:
                y_ref[...] = part
            else:
                y_ref[...] += part


def _moe(l, blk_exp, n_active, xb, wg, wu, wd):
    def row_blk(i, be, na):
        return (jnp.minimum(i, na[0] - 1), 0)

    def w_blk(i, be, na):
        return (l, be[i], 0, 0)

    return pl.pallas_call(
        _moe_body,
        grid_spec=pltpu.PrefetchScalarGridSpec(
            num_scalar_prefetch=2,
            grid=(N_BLOCKS,),
            in_specs=[
                pl.BlockSpec((TM_MOE, D_HALF), row_blk),
                pl.BlockSpec((None, None, D_MODEL, D_EXPERT), w_blk),
                pl.BlockSpec((None, None, D_MODEL, D_EXPERT), w_blk),
                pl.BlockSpec((None, None, D_EXPERT, D_MODEL), w_blk),
            ],
            out_specs=pl.BlockSpec((TM_MOE, D_MODEL), lambda i, be, na: (i, 0)),
            scratch_shapes=[
                pltpu.VMEM((D_MODEL, D_EXPERT), BF16),
                pltpu.VMEM((D_MODEL, D_EXPERT), BF16),
                pltpu.VMEM((D_EXPERT, D_MODEL), BF16),
            ],
        ),
        out_shape=jax.ShapeDtypeStruct((N_ROWS, D_MODEL), F32),
        compiler_params=_params(("arbitrary",), vmem=60 * 1024 * 1024),
        name="moe",
    )(blk_exp, n_active, xb, wg, wu, wd)


def _combine_body(dest_ref, x_ref, rg_ref, gfin_ref, yb_ref, o_ref, ybuf, sem, *, final):
    tc = x_ref.shape[0]
    base = pl.program_id(0) * tc

    def row_copy(r, k):
        d = dest_ref[2 * (base + r) + k]
        return pltpu.make_async_copy(yb_ref.at[pl.ds(d, 1), :], ybuf.at[k, pl.ds(r, 1), :], sem)

    def issue(r, c):
        row_copy(r, 0).start()
        row_copy(r, 1).start()
        return c

    def drain(r, c):
        row_copy(r, 0).wait()
        row_copy(r, 1).wait()
        return c

    lax.fori_loop(0, tc, issue, 0)
    lax.fori_loop(0, tc, drain, 0)

    rg = rg_ref[...]
    y = x_ref[...] + rg[:, 0:1] * ybuf[0] + rg[:, 1:2] * ybuf[1]
    if final:
        ms = jnp.mean(y * y, axis=-1, keepdims=True)
        y = (y * lax.rsqrt(ms + EPS)) * gfin_ref[...]
    o_ref[...] = y


def _combine(dest, x, rg, gfin, yb, final):
    tc = 256
    return pl.pallas_call(
        functools.partial(_combine_body, final=final),
        grid_spec=pltpu.PrefetchScalarGridSpec(
            num_scalar_prefetch=1,
            grid=(N_TOK // tc,),
            in_specs=[
                pl.BlockSpec((tc, D_MODEL), lambda i, d: (i, 0)),
                pl.BlockSpec((tc, LANES), lambda i, d: (i, 0)),
                pl.BlockSpec((1, D_MODEL), lambda i, d: (0, 0)),
                pl.BlockSpec(memory_space=pl.ANY),
            ],
            out_specs=pl.BlockSpec((tc, D_MODEL), lambda i, d: (i, 0)),
            scratch_shapes=[pltpu.VMEM((2, tc, D_MODEL), F32), pltpu.SemaphoreType.DMA(())],
        ),
        out_shape=jax.ShapeDtypeStruct((N_TOK, D_MODEL), F32),
        compiler_params=_params(("arbitrary",)),
        name="combine",
    )(dest, x, rg, gfin, yb)


def _plan(ri, cnt):
    counts = cnt[0, ROUTE_OFF:ROUTE_OFF + N_EXPERTS].astype(jnp.int32)
    nblk = (counts + TM_MOE - 1) // TM_MOE
    blk_end = jnp.cumsum(nblk)
    pad_start = (blk_end - nblk) * TM_MOE
    n_active = blk_end[-1:]
    blk = jnp.minimum(jnp.arange(N_BLOCKS, dtype=jnp.int32), n_active[0] - 1)
    blk_exp = jnp.sum(blk_end[None, :] <= blk[:, None], axis=1)
    eid, rank = ri[:, 0:2], ri[:, 2:4]
    onehot = eid[:, :, None] == jnp.arange(N_EXPERTS, dtype=jnp.int32)[None, None, :]
    dest = rank + jnp.sum(jnp.where(onehot, pad_start[None, None, :], 0), axis=-1)
    return dest.reshape(-1).astype(jnp.int32), blk_exp.astype(jnp.int32), n_active.astype(jnp.int32)


def kernel(x, mix_norm, w_in, w_fourier, w_spatial, b_spatial, conv_w, w_pool, pool_scale,
           out_norm, w_out, ffn_norm, w_group, b_group, w_router, b_router, w_gate, w_up,
           w_down, final_norm):
    cs, m2 = _fourier_consts()
    tr = 512
    tri = (jnp.arange(tr)[:, None] > jnp.arange(tr)[None, :]).astype(BF16)

    w_in_b = w_in.astype(BF16)
    w_f_b = w_fourier.astype(BF16)
    w_s_b = w_spatial.astype(BF16)
    w_p_b = w_pool.astype(BF16)
    w_o_b = w_out.astype(BF16)
    pad = LANES - N_EXPERT_GROUPS - N_EXPERTS
    wr = jnp.concatenate([w_group, w_router, jnp.zeros((DEPTH, D_MODEL, pad), F32)], axis=-1)
    br = jnp.concatenate([b_group, b_router, jnp.zeros((DEPTH, pad), F32)], axis=-1)[:, None, :]
    bsb = jnp.broadcast_to(b_spatial[:, :, :, None], (DEPTH, N_HEADS, CHUNK, HEAD_DIM))
    xb0 = jnp.zeros((N_ROWS, D_HALF), jnp.uint32)
    mix_g = mix_norm[:, None, :]
    ffn_g = ffn_norm[:, None, :]
    out_g = out_norm.reshape(DEPTH, 1, D_MODEL)
    pool_s = pool_scale[:, None, :]

    xt = x.reshape(N_TOK, D_MODEL)
    for l in range(DEPTH):
        z = _inproj(l, xt, mix_g, w_in_b)
        z3 = z.reshape(BATCH, SEQ, D_IN_PROJ)
        f3 = _fourier(z3, cs, m2)
        x3 = _mixer(l, xt.reshape(BATCH, SEQ, D_MODEL), f3, z3, w_f_b, w_s_b, bsb,
                    conv_w, w_p_b, pool_s, out_g, w_o_b)
        xt = x3.reshape(N_TOK, D_MODEL)
        hp, ri, rg, cnt = _router(l, xt, ffn_g, wr, br, tri)
        dest, blk_exp, n_active = _plan(ri, cnt)
        xb = _dispatch(dest, hp, xb0)
        yb = _moe(l, blk_exp, n_active, xb, w_gate, w_up, w_down)
        xt = _combine(dest, xt, rg, final_norm[None, :], yb, final=(l == DEPTH - 1))
    return xt.reshape(BATCH, SEQ, D_MODEL)
```

```python
import functools

import numpy as np
import jax
import jax.numpy as jnp
from jax import lax
from jax.experimental import pallas as pl
from jax.experimental.pallas import tpu as pltpu

D_MODEL = 2048
BATCH = 4
SEQ = 4096
DEPTH = 4
N_TOK = BATCH * SEQ
D_GROUP = 512
HEAD_DIM = 128
N_HEADS = 4
CHUNK = 128
POOL_WINDOWS = (2, 4, 8, 16)
D_IN_PROJ = 7 * D_GROUP
N_EXPERT_GROUPS = 4
EXPERTS_PER_GROUP = 8
N_EXPERTS = 32
D_EXPERT = 768
EPS = 1e-6

F32 = jnp.float32
BF16 = jnp.bfloat16

FFT_Q = 8
FFT_P = SEQ // FFT_Q
FFT_HEADS = 2
FFT_W = FFT_HEADS * HEAD_DIM
HALO = 8
LANES = 128
ROUTE_OFF = N_EXPERT_GROUPS

TM_MOE = 256
N_BLOCKS = (2 * N_TOK) // TM_MOE + N_EXPERTS
N_ROWS = N_BLOCKS * TM_MOE
D_HALF = D_MODEL // 2
ROW_TILE = D_HALF // LANES

VMEM_LIMIT = 56 * 1024 * 1024


def _params(sem, vmem=VMEM_LIMIT):
    return pltpu.CompilerParams(dimension_semantics=sem, vmem_limit_bytes=vmem)


def _unpack_pairs(u):
    lo = pltpu.bitcast(u << 16, F32)
    hi = pltpu.bitcast(u & jnp.uint32(0xFFFF0000), F32)
    return jnp.concatenate([lo, hi], axis=1)


def _pack_pairs(v):
    bits = pltpu.bitcast(v.astype(BF16).astype(F32), jnp.uint32)
    return (bits[:, :D_HALF] >> 16) | (bits[:, D_HALF:] & jnp.uint32(0xFFFF0000))


def _store_row_tiles(ref, packed):
    rows = packed.shape[0]
    for s in range(ROW_TILE):
        ref[pl.ds(s, rows, stride=ROW_TILE), :] = packed[:, s * LANES:(s + 1) * LANES]


def _load_row_tiles(ref, rows):
    return jnp.concatenate(
        [ref[pl.ds(s, rows, stride=ROW_TILE), :] for s in range(ROW_TILE)], axis=1)


def _moe_residual(x_ref, y0_ref, y1_ref, rg_ref):
    rg = rg_ref[...]
    rows = x_ref.shape[0]
    return (x_ref[...] + rg[:, 0:1] * _unpack_pairs(_load_row_tiles(y0_ref, rows))
            + rg[:, 1:2] * _unpack_pairs(_load_row_tiles(y1_ref, rows)))


def _project(x, g_ref, w_ref, z_ref):
    ms = jnp.mean(x * x, axis=-1, keepdims=True)
    h = ((x * lax.rsqrt(ms + EPS)) * g_ref[...]).astype(BF16)
    tn = 512
    for j in range(D_IN_PROJ // tn):
        cs = slice(j * tn, (j + 1) * tn)
        z_ref[:, cs] = jnp.dot(h, w_ref[:, cs], preferred_element_type=F32)


def _inproj_body(x_ref, g_ref, w_ref, z_ref):
    _project(x_ref[...], g_ref, w_ref, z_ref)


def _inproj_moe_body(x_ref, y0_ref, y1_ref, rg_ref, g_ref, w_ref, z_ref, xo_ref):
    x = _moe_residual(x_ref, y0_ref, y1_ref, rg_ref)
    xo_ref[...] = x
    _project(x, g_ref, w_ref, z_ref)


def _inproj(l, x, g, w, moe=None):
    tm = 512 if moe is None else 256
    row = pl.BlockSpec((tm, D_MODEL), lambda i: (i, 0))
    w_specs = [
        pl.BlockSpec((None, 1, D_MODEL), lambda i: (l, 0, 0)),
        pl.BlockSpec((None, D_MODEL, D_IN_PROJ), lambda i: (l, 0, 0), pipeline_mode=pl.Buffered(1)),
    ]
    z_spec = pl.BlockSpec((tm, D_IN_PROJ), lambda i: (i, 0))
    z_shape = jax.ShapeDtypeStruct((N_TOK, D_IN_PROJ), F32)
    if moe is None:
        return pl.pallas_call(
            _inproj_body, grid=(N_TOK // tm,), in_specs=[row] + w_specs, out_specs=z_spec,
            out_shape=z_shape, compiler_params=_params(("parallel",)), name="inproj",
        )(x, g, w)
    yk, rg = moe
    return pl.pallas_call(
        _inproj_moe_body,
        grid=(N_TOK // tm,),
        in_specs=[
            row,
            pl.BlockSpec((None, tm * ROW_TILE, LANES), lambda i: (0, i, 0)),
            pl.BlockSpec((None, tm * ROW_TILE, LANES), lambda i: (1, i, 0)),
            pl.BlockSpec((tm, LANES), lambda i: (i, 0)),
        ] + w_specs,
        out_specs=[z_spec, row],
        out_shape=[z_shape, jax.ShapeDtypeStruct((N_TOK, D_MODEL), F32)],
        compiler_params=_params(("parallel",)),
        name="inproj_moe",
    )(x, yk, yk, rg, g, w)


def _fourier_consts():
    c = np.arange(HEAD_DIM)
    ang = 2.0 * np.pi * (np.outer(c, c) % HEAD_DIM) / HEAD_DIM
    cc, sc = np.cos(ang), np.sin(ang)
    scale = 1.0 / np.sqrt(SEQ * HEAD_DIM)
    cs = np.block([[cc, -sc], [sc, cc]]) * scale
    s1 = np.arange(FFT_P)
    m2 = np.zeros((FFT_Q, FFT_P, 2 * FFT_P), np.float64)
    for k2 in range(FFT_Q):
        k = FFT_Q * np.arange(FFT_P) + k2
        th = 2.0 * np.pi * (np.outer(k, s1) % SEQ) / SEQ
        m2[k2, :, :FFT_P] = np.cos(th)
        m2[k2, :, FFT_P:] = np.sin(th)
    return jnp.asarray(cs, BF16), jnp.asarray(m2, BF16)


def _fourier_body(z_ref, cs_ref, m2_ref, f_ref, y_scr, u_scr):
    rc = 64
    r = np.float32(np.sqrt(0.5))

    def chunk(ci, carry):
        r0 = pl.multiple_of(ci * rc, rc)
        zb = [z_ref[pl.ds(s2 * FFT_P + r0, rc), :] for s2 in range(FFT_Q)]
        e0, e1 = zb[0] + zb[4], zb[0] - zb[4]
        e2, e3 = zb[2] + zb[6], zb[2] - zb[6]
        o0, o1 = zb[1] + zb[5], zb[1] - zb[5]
        o2, o3 = zb[3] + zb[7], zb[3] - zb[7]
        p, q = (o1 - o3) * r, (o1 + o3) * r
        ee, oo = e0 + e2, o0 + o2
        ed, od = e0 - e2, o0 - o2
        zero = jnp.zeros_like(e0)
        ys = [(ee + oo, zero), (e1 + p, -e3 - q), (ed, -od), (e1 - p, e3 - q),
              (ee - oo, zero), (e1 - p, q - e3), (ed, od), (e1 + p, e3 + q)]
        for k2 in range(FFT_Q):
            re, im = ys[k2]
            rows = pl.ds(k2 * FFT_P + r0, rc)
            for h in range(FFT_HEADS):
                hs = slice(h * HEAD_DIM, (h + 1) * HEAD_DIM)
                y_scr[rows, 2 * h * HEAD_DIM:(2 * h + 1) * HEAD_DIM] = re[:, hs].astype(BF16)
                y_scr[rows, (2 * h + 1) * HEAD_DIM:(2 * h + 2) * HEAD_DIM] = im[:, hs].astype(BF16)
        return carry

    lax.fori_loop(0, FFT_P // rc, chunk, 0)

    for k2 in range(FFT_Q):
        for h in range(FFT_HEADS):
            yk = y_scr[k2 * FFT_P:(k2 + 1) * FFT_P, 2 * h * HEAD_DIM:(2 * h + 2) * HEAD_DIM]
            ab = jnp.dot(yk, cs_ref[...], preferred_element_type=F32)
            hs = slice(h * HEAD_DIM, (h + 1) * HEAD_DIM)
            u_scr[k2, 0:FFT_P, hs] = ab[:, :HEAD_DIM].astype(BF16)
            u_scr[k2, FFT_P:2 * FFT_P, hs] = ab[:, HEAD_DIM:].astype(BF16)

    for k2 in range(FFT_Q):
        res = jnp.dot(m2_ref[k2], u_scr[k2], preferred_element_type=F32)
        for h in range(FFT_HEADS):
            f_ref[h, pl.ds(k2, FFT_P, stride=FFT_Q), :] = res[:, h * HEAD_DIM:(h + 1) * HEAD_DIM]


def _fourier(z3, cs, m2):
    return pl.pallas_call(
        _fourier_body,
        grid=(BATCH, N_HEADS // FFT_HEADS),
        in_specs=[
            pl.BlockSpec((None, SEQ, FFT_W), lambda b, h: (b, 0, h)),
            pl.BlockSpec((2 * HEAD_DIM, 2 * HEAD_DIM), lambda b, h: (0, 0)),
            pl.BlockSpec((FFT_Q, FFT_P, 2 * FFT_P), lambda b, h: (0, 0, 0)),
        ],
        out_specs=pl.BlockSpec((None, FFT_HEADS, SEQ, HEAD_DIM), lambda b, h: (b, h, 0, 0)),
        out_shape=jax.ShapeDtypeStruct((BATCH, N_HEADS, SEQ, HEAD_DIM), F32),
        scratch_shapes=[
            pltpu.VMEM((SEQ, 2 * FFT_W), BF16),
            pltpu.VMEM((FFT_Q, 2 * FFT_P, FFT_W), BF16),
        ],
        compiler_params=_params(("parallel", "parallel")),
        name="fourier",
    )(z3, cs, m2)


def _gelu(x):
    return 0.5 * x * (1.0 + lax.erf(x * np.float32(np.sqrt(0.5))))


def _mixer_body(x_ref, f_ref, zu_ref, zv_ref, zb_ref, zc_ref, zval_ref, zp_ref,
                zc_prev, zval_prev, zp_prev, zc_next, zval_next, zp_next,
                wf_ref, ws_ref, bs_ref, cw_ref, wp_ref, ps_ref, on_ref, wo_ref,
                o_ref, ext_a, ext_p, ybf):
    i = pl.program_id(1)
    ts = x_ref.shape[0]
    keep_prev = (i > 0).astype(F32)
    keep_next = (i < pl.num_programs(1) - 1).astype(F32)

    def norm_store(y, g):
        gs = slice(g * D_GROUP, (g + 1) * D_GROUP)
        ms = jnp.mean(y * y, axis=-1, keepdims=True)
        ybf[:, gs] = ((y * lax.rsqrt(ms + EPS)) * on_ref[:, gs]).astype(BF16)

    spec = jnp.concatenate([f_ref[h] for h in range(N_HEADS)], axis=1).astype(BF16)
    norm_store(jnp.dot(spec, wf_ref[...], preferred_element_type=F32), 0)

    gu = _gelu(zu_ref[...])
    gv = _gelu(zv_ref[...]).astype(BF16)
    cols = []
    for h in range(N_HEADS):
        hs = slice(h * HEAD_DIM, (h + 1) * HEAD_DIM)
        rows = []
        for n in range(ts // CHUNK):
            vv = gv[n * CHUNK:(n + 1) * CHUNK, hs]
            rows.append(jnp.dot(ws_ref[h], vv, preferred_element_type=F32) + bs_ref[h])
        cols.append(jnp.concatenate(rows, axis=0))
    norm_store(gu * jnp.concatenate(cols, axis=1), 1)

    a = zc_ref[...] * zval_ref[...]
    ext_a[0:HALO, :] = zc_prev[...] * zval_prev[...] * keep_prev
    ext_a[HALO:HALO + ts, :] = a
    ext_a[HALO + ts:2 * HALO + ts, :] = zc_next[...] * zval_next[...] * keep_next
    conv = (cw_ref[0:1, :] * ext_a[HALO - 1:HALO - 1 + ts, :] + cw_ref[1:2, :] * a
            + cw_ref[2:3, :] * ext_a[HALO + 1:HALO + 1 + ts, :])
    norm_store(zb_ref[...] * conv, 2)

    zp = zp_ref[...]
    ext_p[0:HALO, :] = zp_prev[...] * keep_prev
    ext_p[HALO:HALO + ts, :] = zp
    ext_p[HALO + ts:2 * HALO + ts, :] = zp_next[...] * keep_next
    t = i * ts + lax.broadcasted_iota(jnp.int32, (ts, HEAD_DIM), 0)
    outs = []
    for g, w in enumerate(POOL_WINDOWS):
        gs = slice(g * HEAD_DIM, (g + 1) * HEAD_DIM)
        acc = ext_p[HALO - w // 2:HALO - w // 2 + ts, gs]
        for d in range(-w // 2 + 1, w // 2):
            acc = acc + ext_p[HALO + d:HALO + d + ts, gs]
        cnt = (jnp.minimum(t + w // 2, SEQ) - jnp.maximum(t - w // 2, 0)).astype(F32)
        pg = acc / cnt - zp[:, gs]
        outs.append(jnp.dot(pg.astype(BF16), wp_ref[g], preferred_element_type=F32))
    norm_store(jnp.concatenate(outs, axis=1) * ps_ref[...], 3)

    o_ref[...] = x_ref[...] + jnp.dot(ybf[...], wo_ref[...], preferred_element_type=F32)


def _mixer(l, x3, f3, z3, wf, ws, bsb, cw, wp, ps, on, wo):
    ts = 512
    nb8 = ts // HALO
    last8 = SEQ // HALO - 1

    def col(j):
        return pl.BlockSpec((None, ts, D_GROUP), lambda b, i, j=j: (b, i, j))

    def prev(j):
        return pl.BlockSpec((None, HALO, D_GROUP),
                            lambda b, i, j=j: (b, jnp.maximum(i * nb8 - 1, 0), j))

    def nxt(j):
        return pl.BlockSpec((None, HALO, D_GROUP),
                            lambda b, i, j=j: (b, jnp.minimum((i + 1) * nb8, last8), j))

    def full(shape, **kw):
        return pl.BlockSpec((None,) + shape, lambda b, i: (l,) + (0,) * len(shape), **kw)

    return pl.pallas_call(
        _mixer_body,
        grid=(BATCH, SEQ // ts),
        in_specs=[
            pl.BlockSpec((None, ts, D_MODEL), lambda b, i: (b, i, 0)),
            pl.BlockSpec((None, N_HEADS, ts, HEAD_DIM), lambda b, i: (b, 0, i, 0)),
            col(1), col(2), col(3), col(4), col(5), col(6),
            prev(4), prev(5), prev(6), nxt(4), nxt(5), nxt(6),
            full((D_GROUP, D_GROUP)), full((N_HEADS, CHUNK, CHUNK)),
            full((N_HEADS, CHUNK, HEAD_DIM)), full((3, D_GROUP)),
            full((4, HEAD_DIM, HEAD_DIM)), full((1, D_GROUP)), full((1, D_MODEL)),
            full((D_MODEL, D_MODEL), pipeline_mode=pl.Buffered(1)),
        ],
        out_specs=pl.BlockSpec((None, ts, D_MODEL), lambda b, i: (b, i, 0)),
        out_shape=jax.ShapeDtypeStruct((BATCH, SEQ, D_MODEL), F32),
        scratch_shapes=[
            pltpu.VMEM((ts + 2 * HALO, D_GROUP), F32),
            pltpu.VMEM((ts + 2 * HALO, D_GROUP), F32),
            pltpu.VMEM((ts, D_MODEL), BF16),
        ],
        compiler_params=_params(("parallel", "arbitrary")),
        name="mixer",
    )(x3, f3, z3, z3, z3, z3, z3, z3, z3, z3, z3, z3, z3, z3, wf, ws, bsb, cw, wp, ps, on, wo)


def _router_body(x_ref, g_ref, wr_ref, br_ref, tri_ref, hp_ref, ri_ref, rg_ref, cnt_ref, carry):
    @pl.when(pl.program_id(0) == 0)
    def _():
        carry[...] = jnp.zeros_like(carry)

    x = x_ref[...]
    tr = x.shape[0]
    ms = jnp.mean(x * x, axis=-1, keepdims=True)
    h = (x * lax.rsqrt(ms + EPS)) * g_ref[...]

    _store_row_tiles(hp_ref, _pack_pairs(h))

    logits = jnp.dot(h, wr_ref[...], precision=lax.Precision.HIGHEST,
                     preferred_element_type=F32) + br_ref[...]
    lane = lax.broadcasted_iota(jnp.int32, (tr, LANES), 1)
    neg = np.float32(-np.inf)

    def first_argmax(v, vmax):
        return jnp.min(jnp.where(v == vmax, lane, LANES), axis=-1, keepdims=True)

    mg = jnp.where(lane < N_EXPERT_GROUPS, logits, neg)
    gmax = jnp.max(mg, axis=-1, keepdims=True)
    grp = first_argmax(mg, gmax)
    p_grp = 1.0 / jnp.sum(jnp.exp(mg - gmax), axis=-1, keepdims=True)

    lo = ROUTE_OFF + EXPERTS_PER_GROUP * grp
    le = jnp.where(lane >= lo, jnp.where(lane < lo + EXPERTS_PER_GROUP, logits, neg), neg)
    emax = jnp.max(le, axis=-1, keepdims=True)
    i1 = first_argmax(le, emax)
    le2 = jnp.where(lane == i1, neg, le)
    emax2 = jnp.max(le2, axis=-1, keepdims=True)
    i2 = first_argmax(le2, emax2)
    e2 = jnp.exp(emax2 - emax)
    gate1 = p_grp / (1.0 + e2)
    gate2 = p_grp * e2 / (1.0 + e2)

    is1, is2 = lane == i1, lane == i2
    oh = jnp.where(is1, 1.0, jnp.where(is2, 1.0, 0.0))
    pref = jnp.dot(tri_ref[...], oh.astype(BF16), preferred_element_type=F32) + carry[...]
    r1 = jnp.sum(jnp.where(is1, pref, 0.0), axis=-1, keepdims=True).astype(jnp.int32)
    r2 = jnp.sum(jnp.where(is2, pref, 0.0), axis=-1, keepdims=True).astype(jnp.int32)
    carry[...] = carry[...] + jnp.sum(oh, axis=0, keepdims=True)
    cnt_ref[...] = carry[...]

    ri_ref[...] = jnp.where(lane == 0, i1 - ROUTE_OFF,
                            jnp.where(lane == 1, i2 - ROUTE_OFF,
                                      jnp.where(lane == 2, r1, jnp.where(lane == 3, r2, 0))))
    rg_ref[...] = jnp.where(lane == 0, gate1, jnp.where(lane == 1, gate2, 0.0))


def _router(l, x, g, wr, br, tri):
    tr = tri.shape[0]
    return pl.pallas_call(
        _router_body,
        grid=(N_TOK // tr,),
        in_specs=[
            pl.BlockSpec((tr, D_MODEL), lambda i: (i, 0)),
            pl.BlockSpec((None, 1, D_MODEL), lambda i: (l, 0, 0)),
            pl.BlockSpec((None, D_MODEL, LANES), lambda i: (l, 0, 0)),
            pl.BlockSpec((None, 1, LANES), lambda i: (l, 0, 0)),
            pl.BlockSpec((tr, tr), lambda i: (0, 0)),
        ],
        out_specs=[
            pl.BlockSpec((tr * ROW_TILE, LANES), lambda i: (i, 0)),
            pl.BlockSpec((tr, LANES), lambda i: (i, 0)),
            pl.BlockSpec((tr, LANES), lambda i: (i, 0)),
            pl.BlockSpec((1, LANES), lambda i: (0, 0)),
        ],
        out_shape=[
            jax.ShapeDtypeStruct((N_TOK * ROW_TILE, LANES), jnp.uint32),
            jax.ShapeDtypeStruct((N_TOK, LANES), jnp.int32),
            jax.ShapeDtypeStruct((N_TOK, LANES), F32),
            jax.ShapeDtypeStruct((1, LANES), F32),
        ],
        scratch_shapes=[pltpu.VMEM((1, LANES), F32)],
        compiler_params=_params(("arbitrary",)),
        name="router",
    )(x, g, wr, br, tri)


def _moe_body(be_ref, na_ref, nv_ref, gs_ref, sd_ref, hp_ref, wg32_ref, wu32_ref, wd32_ref, yk_ref,
              wg_ref, wu_ref, wd_ref, xbuf, ybuf, yacc, gsem, ssem):
    i = pl.program_id(0)
    na = na_ref[0]
    slot = lax.rem(i, 2)
    other = 1 - slot
    blk_rows = TM_MOE * ROW_TILE

    def tile_rows(r):
        start = r * ROW_TILE
        return pl.ds(start if isinstance(r, int) else pl.multiple_of(start, ROW_TILE), ROW_TILE)

    def gather_copy(blk, s, r):
        src = pl.multiple_of(gs_ref[blk * TM_MOE + r], ROW_TILE)
        return pltpu.make_async_copy(hp_ref.at[pl.ds(src, ROW_TILE), :],
                                     xbuf.at[s, tile_rows(r), :], gsem.at[s])

    def scatter_copy(blk, s, r):
        dst = pl.multiple_of(sd_ref[blk * TM_MOE + r], ROW_TILE)
        return pltpu.make_async_copy(ybuf.at[s, tile_rows(r), :],
                                     yk_ref.at[pl.ds(dst, ROW_TILE), :], ssem.at[s])

    def wait_gather(s):
        pltpu.make_async_copy(hp_ref.at[pl.ds(0, blk_rows), :], xbuf.at[s], gsem.at[s]).wait()

    def wait_scatter(blk, s):
        n = nv_ref[blk]

        @pl.when(n == TM_MOE)
        def _():
            pltpu.make_async_copy(ybuf.at[s], yk_ref.at[pl.ds(0, blk_rows), :], ssem.at[s]).wait()

        @pl.when(n < TM_MOE)
        def _():
            lax.fori_loop(0, n, lambda r, c: (scatter_copy(blk, s, r).wait(), c)[1], 0)

    @pl.when(i == 0)
    def _():
        lax.fori_loop(0, TM_MOE, lambda r, c: (gather_copy(0, 0, r).start(), c)[1], 0)

    @pl.when((i < na) & ((i == 0) | (be_ref[i] != be_ref[jnp.maximum(i - 1, 0)])))
    def _():
        wg_ref[...] = wg32_ref[...].astype(BF16)
        wu_ref[...] = wu32_ref[...].astype(BF16)
        wd_ref[...] = wd32_ref[...].astype(BF16)

    @pl.when((i >= 2) & (i < na))
    def _():
        wait_scatter(i - 2, slot)

    @pl.when(i < na)
    def _():
        wait_gather(slot)
        nxt = jnp.minimum(i + 1, na - 1)
        for r in range(TM_MOE):
            gather_copy(nxt, other, r).start()
        xu = _load_row_tiles(xbuf.at[slot], TM_MOE)
        lo = pltpu.bitcast(xu << 16, F32).astype(BF16)
        hi = pltpu.bitcast(xu & jnp.uint32(0xFFFF0000), F32).astype(BF16)
        fc = 256
        for c in range(D_EXPERT // fc):
            cs = slice(c * fc, (c + 1) * fc)
            g = (jnp.dot(lo, wg_ref[0:D_HALF, cs], preferred_element_type=F32)
                 + jnp.dot(hi, wg_ref[D_HALF:D_MODEL, cs], preferred_element_type=F32))
            u = (jnp.dot(lo, wu_ref[0:D_HALF, cs], preferred_element_type=F32)
                 + jnp.dot(hi, wu_ref[D_HALF:D_MODEL, cs], preferred_element_type=F32))
            a = (g * jax.nn.sigmoid(g) * u).astype(BF16)
            part = jnp.dot(a, wd_ref[cs, :], preferred_element_type=F32)
            if c == 0:
                yacc[...] = part
            else:
                yacc[...] += part
        _store_row_tiles(ybuf.at[slot], _pack_pairs(yacc[...]))

    nv = nv_ref[jnp.minimum(i, na - 1)]

    @pl.when((i < na) & (nv == TM_MOE))
    def _():
        for r in range(TM_MOE):
            scatter_copy(i, slot, r).start()

    @pl.when((i < na) & (nv < TM_MOE))
    def _():
        lax.fori_loop(0, nv, lambda r, c: (scatter_copy(i, slot, r).start(), c)[1], 0)

    @pl.when(i == na - 1)
    def _():
        wait_gather(other)
        wait_scatter(i, slot)

        @pl.when(i >= 1)
        def _():
            wait_scatter(i - 1, other)


def _moe(l, blk_exp, n_active, n_valid, gsrc, sdst, hp, wg, wu, wd):
    def w_blk(i, be, na, nv, gs, sd):
        return (l, be[i], 0, 0)

    return pl.pallas_call(
        _moe_body,
        grid_spec=pltpu.PrefetchScalarGridSpec(
            num_scalar_prefetch=5,
            grid=(N_BLOCKS,),
            in_specs=[
                pl.BlockSpec(memory_space=pl.ANY),
                pl.BlockSpec((None, None, D_MODEL, D_EXPERT), w_blk),
                pl.BlockSpec((None, None, D_MODEL, D_EXPERT), w_blk),
                pl.BlockSpec((None, None, D_EXPERT, D_MODEL), w_blk),
            ],
            out_specs=pl.BlockSpec(memory_space=pl.ANY),
            scratch_shapes=[
                pltpu.VMEM((D_MODEL, D_EXPERT), BF16),
                pltpu.VMEM((D_MODEL, D_EXPERT), BF16),
                pltpu.VMEM((D_EXPERT, D_MODEL), BF16),
                pltpu.VMEM((2, TM_MOE * ROW_TILE, LANES), jnp.uint32),
                pltpu.VMEM((2, TM_MOE * ROW_TILE, LANES), jnp.uint32),
                pltpu.VMEM((TM_MOE, D_MODEL), F32),
                pltpu.SemaphoreType.DMA((2,)),
                pltpu.SemaphoreType.DMA((2,)),
            ],
        ),
        out_shape=jax.ShapeDtypeStruct((2 * N_TOK * ROW_TILE, LANES), jnp.uint32),
        compiler_params=_params(("arbitrary",), vmem=60 * 1024 * 1024),
        name="moe",
    )(blk_exp, n_active, n_valid, gsrc, sdst, hp, wg, wu, wd)


def _final_body(x_ref, y0_ref, y1_ref, rg_ref, g_ref, o_ref):
    y = _moe_residual(x_ref, y0_ref, y1_ref, rg_ref)
    ms = jnp.mean(y * y, axis=-1, keepdims=True)
    o_ref[...] = (y * lax.rsqrt(ms + EPS)) * g_ref[...]


def _final(x, yk, rg, g):
    tm = 512
    row = pl.BlockSpec((tm, D_MODEL), lambda i: (i, 0))
    return pl.pallas_call(
        _final_body,
        grid=(N_TOK // tm,),
        in_specs=[
            row,
            pl.BlockSpec((None, tm * ROW_TILE, LANES), lambda i: (0, i, 0)),
            pl.BlockSpec((None, tm * ROW_TILE, LANES), lambda i: (1, i, 0)),
            pl.BlockSpec((tm, LANES), lambda i: (i, 0)),
            pl.BlockSpec((1, D_MODEL), lambda i: (0, 0)),
        ],
        out_specs=row,
        out_shape=jax.ShapeDtypeStruct((N_TOK, D_MODEL), F32),
        compiler_params=_params(("parallel",)),
        name="final",
    )(x, yk, yk, rg, g)


def _plan(ri, cnt):
    counts = cnt[0, ROUTE_OFF:ROUTE_OFF + N_EXPERTS].astype(jnp.int32)
    nblk = (counts + TM_MOE - 1) // TM_MOE
    blk_end = jnp.cumsum(nblk)
    blk_start = blk_end - nblk
    pad_start = blk_start * TM_MOE
    n_active = blk_end[-1:]
    blk = jnp.minimum(jnp.arange(N_BLOCKS, dtype=jnp.int32), n_active[0] - 1)
    blk_exp = jnp.sum(blk_end[None, :] <= blk[:, None], axis=1)
    n_valid = jnp.clip(counts[blk_exp] - (blk - blk_start[blk_exp]) * TM_MOE, 0, TM_MOE)
    eid, rank = ri[:, 0:2], ri[:, 2:4]
    onehot = eid[:, :, None] == jnp.arange(N_EXPERTS, dtype=jnp.int32)[None, None, :]
    dest = rank + jnp.sum(jnp.where(onehot, pad_start[None, None, :], 0), axis=-1)
    row_asg = jnp.zeros((N_ROWS,), jnp.int32).at[dest.reshape(-1)].set(
        jnp.arange(2 * N_TOK, dtype=jnp.int32), unique_indices=True)
    tok, k = row_asg >> 1, row_asg & 1
    gsrc = tok * ROW_TILE
    sdst = (k * N_TOK + tok) * ROW_TILE
    return (blk_exp.astype(jnp.int32), n_active.astype(jnp.int32), n_valid.astype(jnp.int32),
            gsrc, sdst)


def kernel(x, mix_norm, w_in, w_fourier, w_spatial, b_spatial, conv_w, w_pool, pool_scale,
           out_norm, w_out, ffn_norm, w_group, b_group, w_router, b_router, w_gate, w_up,
           w_down, final_norm):
    cs, m2 = _fourier_consts()
    tr = 512
    tri = (jnp.arange(tr)[:, None] > jnp.arange(tr)[None, :]).astype(BF16)

    w_in_b = w_in.astype(BF16)
    w_f_b = w_fourier.astype(BF16)
    w_s_b = w_spatial.astype(BF16)
    w_p_b = w_pool.astype(BF16)
    w_o_b = w_out.astype(BF16)
    pad = LANES - N_EXPERT_GROUPS - N_EXPERTS
    wr = jnp.concatenate([w_group, w_router, jnp.zeros((DEPTH, D_MODEL, pad), F32)], axis=-1)
    br = jnp.concatenate([b_group, b_router, jnp.zeros((DEPTH, pad), F32)], axis=-1)[:, None, :]
    bsb = jnp.broadcast_to(b_spatial[:, :, :, None], (DEPTH, N_HEADS, CHUNK, HEAD_DIM))
    mix_g = mix_norm[:, None, :]
    ffn_g = ffn_norm[:, None, :]
    out_g = out_norm.reshape(DEPTH, 1, D_MODEL)
    pool_s = pool_scale[:, None, :]

    xt = x.reshape(N_TOK, D_MODEL)
    moe = None
    for l in range(DEPTH):
        if moe is None:
            z = _inproj(l, xt, mix_g, w_in_b)
        else:
            z, xt = _inproj(l, xt, mix_g, w_in_b, moe=moe)
        z3 = z.reshape(BATCH, SEQ, D_IN_PROJ)
        f3 = _fourier(z3, cs, m2)
        x3 = _mixer(l, xt.reshape(BATCH, SEQ, D_MODEL), f3, z3, w_f_b, w_s_b, bsb,
                    conv_w, w_p_b, pool_s, out_g, w_o_b)
        xt = x3.reshape(N_TOK, D_MODEL)
        hp, ri, rg, cnt = _router(l, xt, ffn_g, wr, br, tri)
        blk_exp, n_active, n_valid, gsrc, sdst = _plan(ri, cnt)
        yk = _moe(l, blk_exp, n_active, n_valid, gsrc, sdst, hp, w_gate, w_up, w_down)
        moe = (yk.reshape(2, N_TOK * ROW_TILE, LANES), rg)
    xt = _final(xt, moe[0], moe[1], final_norm[None, :])
    return xt.reshape(BATCH, SEQ, D_MODEL)
```

```python
import functools

import numpy as np
import jax
import jax.numpy as jnp
from jax import lax
from jax.experimental import pallas as pl
from jax.experimental.pallas import tpu as pltpu

D_MODEL = 2048
BATCH = 4
SEQ = 4096
DEPTH = 4
N_TOK = BATCH * SEQ
D_GROUP = 512
HEAD_DIM = 128
N_HEADS = 4
CHUNK = 128
POOL_WINDOWS = (2, 4, 8, 16)
D_IN_PROJ = 7 * D_GROUP
N_EXPERT_GROUPS = 4
EXPERTS_PER_GROUP = 8
N_EXPERTS = 32
D_EXPERT = 768
EPS = 1e-6

F32 = jnp.float32
BF16 = jnp.bfloat16

FFT_Q = 8
FFT_P = SEQ // FFT_Q
FFT_HEADS = 2
FFT_W = FFT_HEADS * HEAD_DIM
HALO = 8
LANES = 128
ROUTE_OFF = N_EXPERT_GROUPS

TM_MOE = 256
N_BLOCKS = (2 * N_TOK) // TM_MOE + N_EXPERTS
N_ROWS = N_BLOCKS * TM_MOE
D_HALF = D_MODEL // 2
ROW_TILE = D_HALF // LANES
ROW_DMA_PRIORITY = 1

VMEM_LIMIT = 56 * 1024 * 1024


def _params(sem, vmem=VMEM_LIMIT):
    return pltpu.CompilerParams(dimension_semantics=sem, vmem_limit_bytes=vmem)


def _unpack_pairs(u):
    lo = pltpu.bitcast(u << 16, F32)
    hi = pltpu.bitcast(u & jnp.uint32(0xFFFF0000), F32)
    return jnp.concatenate([lo, hi], axis=1)


def _pack_pairs(v):
    bits = pltpu.bitcast(v.astype(BF16).astype(F32), jnp.uint32)
    return (bits[:, :D_HALF] >> 16) | (bits[:, D_HALF:] & jnp.uint32(0xFFFF0000))


def _store_row_tiles(ref, packed):
    rows = packed.shape[0]
    for s in range(ROW_TILE):
        ref[pl.ds(s, rows, stride=ROW_TILE), :] = packed[:, s * LANES:(s + 1) * LANES]


def _load_row_tiles(ref, rows):
    return jnp.concatenate(
        [ref[pl.ds(s, rows, stride=ROW_TILE), :] for s in range(ROW_TILE)], axis=1)


def _moe_residual(x_ref, y0_ref, y1_ref, rg_ref):
    rg = rg_ref[...]
    rows = x_ref.shape[0]
    return (x_ref[...] + rg[:, 0:1] * _unpack_pairs(_load_row_tiles(y0_ref, rows))
            + rg[:, 1:2] * _unpack_pairs(_load_row_tiles(y1_ref, rows)))


def _project(x, g_ref, w_ref, z_ref):
    ms = jnp.mean(x * x, axis=-1, keepdims=True)
    h = ((x * lax.rsqrt(ms + EPS)) * g_ref[...]).astype(BF16)
    tn = 512
    for j in range(D_IN_PROJ // tn):
        cs = slice(j * tn, (j + 1) * tn)
        z_ref[:, cs] = jnp.dot(h, w_ref[:, cs], preferred_element_type=F32)


def _inproj_body(x_ref, g_ref, w_ref, z_ref):
    _project(x_ref[...], g_ref, w_ref, z_ref)


def _inproj_moe_body(x_ref, y0_ref, y1_ref, rg_ref, g_ref, w_ref, z_ref, xo_ref):
    x = _moe_residual(x_ref, y0_ref, y1_ref, rg_ref)
    xo_ref[...] = x
    _project(x, g_ref, w_ref, z_ref)


def _inproj(l, x, g, w, moe=None):
    tm = 512 if moe is None else 256
    row = pl.BlockSpec((tm, D_MODEL), lambda i: (i, 0))
    w_specs = [
        pl.BlockSpec((None, 1, D_MODEL), lambda i: (l, 0, 0)),
        pl.BlockSpec((None, D_MODEL, D_IN_PROJ), lambda i: (l, 0, 0), pipeline_mode=pl.Buffered(1)),
    ]
    z_spec = pl.BlockSpec((tm, D_IN_PROJ), lambda i: (i, 0))
    z_shape = jax.ShapeDtypeStruct((N_TOK, D_IN_PROJ), F32)
    if moe is None:
        return pl.pallas_call(
            _inproj_body, grid=(N_TOK // tm,), in_specs=[row] + w_specs, out_specs=z_spec,
            out_shape=z_shape, compiler_params=_params(("parallel",)), name="inproj",
        )(x, g, w)
    yk, rg = moe
    return pl.pallas_call(
        _inproj_moe_body,
        grid=(N_TOK // tm,),
        in_specs=[
            row,
            pl.BlockSpec((None, tm * ROW_TILE, LANES), lambda i: (0, i, 0)),
            pl.BlockSpec((None, tm * ROW_TILE, LANES), lambda i: (1, i, 0)),
            pl.BlockSpec((tm, LANES), lambda i: (i, 0)),
        ] + w_specs,
        out_specs=[z_spec, row],
        out_shape=[z_shape, jax.ShapeDtypeStruct((N_TOK, D_MODEL), F32)],
        compiler_params=_params(("parallel",)),
        name="inproj_moe",
    )(x, yk, yk, rg, g, w)


def _fourier_consts():
    c = np.arange(HEAD_DIM)
    ang = 2.0 * np.pi * (np.outer(c, c) % HEAD_DIM) / HEAD_DIM
    cc, sc = np.cos(ang), np.sin(ang)
    scale = 1.0 / np.sqrt(SEQ * HEAD_DIM)
    cs = np.block([[cc, -sc], [sc, cc]]) * scale
    s1 = np.arange(FFT_P)
    m2 = np.zeros((FFT_Q, FFT_P, 2 * FFT_P), np.float64)
    for k2 in range(FFT_Q):
        k = FFT_Q * np.arange(FFT_P) + k2
        th = 2.0 * np.pi * (np.outer(k, s1) % SEQ) / SEQ
        m2[k2, :, :FFT_P] = np.cos(th)
        m2[k2, :, FFT_P:] = np.sin(th)
    return jnp.asarray(cs, BF16), jnp.asarray(m2, BF16)


def _fourier_body(z_ref, cs_ref, m2_ref, f_ref, y_scr, u_scr):
    rc = 64
    r = np.float32(np.sqrt(0.5))

    def chunk(ci, carry):
        r0 = pl.multiple_of(ci * rc, rc)
        zb = [z_ref[pl.ds(s2 * FFT_P + r0, rc), :] for s2 in range(FFT_Q)]
        e0, e1 = zb[0] + zb[4], zb[0] - zb[4]
        e2, e3 = zb[2] + zb[6], zb[2] - zb[6]
        o0, o1 = zb[1] + zb[5], zb[1] - zb[5]
        o2, o3 = zb[3] + zb[7], zb[3] - zb[7]
        p, q = (o1 - o3) * r, (o1 + o3) * r
        ee, oo = e0 + e2, o0 + o2
        ed, od = e0 - e2, o0 - o2
        zero = jnp.zeros_like(e0)
        ys = [(ee + oo, zero), (e1 + p, -e3 - q), (ed, -od), (e1 - p, e3 - q),
              (ee - oo, zero), (e1 - p, q - e3), (ed, od), (e1 + p, e3 + q)]
        for k2 in range(FFT_Q):
            re, im = ys[k2]
            rows = pl.ds(k2 * FFT_P + r0, rc)
            for h in range(FFT_HEADS):
                hs = slice(h * HEAD_DIM, (h + 1) * HEAD_DIM)
                y_scr[rows, 2 * h * HEAD_DIM:(2 * h + 1) * HEAD_DIM] = re[:, hs].astype(BF16)
                y_scr[rows, (2 * h + 1) * HEAD_DIM:(2 * h + 2) * HEAD_DIM] = im[:, hs].astype(BF16)
        return carry

    lax.fori_loop(0, FFT_P // rc, chunk, 0)

    for k2 in range(FFT_Q):
        for h in range(FFT_HEADS):
            yk = y_scr[k2 * FFT_P:(k2 + 1) * FFT_P, 2 * h * HEAD_DIM:(2 * h + 2) * HEAD_DIM]
            ab = jnp.dot(yk, cs_ref[...], preferred_element_type=F32)
            hs = slice(h * HEAD_DIM, (h + 1) * HEAD_DIM)
            u_scr[k2, 0:FFT_P, hs] = ab[:, :HEAD_DIM].astype(BF16)
            u_scr[k2, FFT_P:2 * FFT_P, hs] = ab[:, HEAD_DIM:].astype(BF16)

    for k2 in range(FFT_Q):
        res = jnp.dot(m2_ref[k2], u_scr[k2], preferred_element_type=F32)
        for h in range(FFT_HEADS):
            f_ref[h, pl.ds(k2, FFT_P, stride=FFT_Q), :] = res[:, h * HEAD_DIM:(h + 1) * HEAD_DIM]


def _fourier(z3, cs, m2):
    return pl.pallas_call(
        _fourier_body,
        grid=(BATCH, N_HEADS // FFT_HEADS),
        in_specs=[
            pl.BlockSpec((None, SEQ, FFT_W), lambda b, h: (b, 0, h)),
            pl.BlockSpec((2 * HEAD_DIM, 2 * HEAD_DIM), lambda b, h: (0, 0)),
            pl.BlockSpec((FFT_Q, FFT_P, 2 * FFT_P), lambda b, h: (0, 0, 0)),
        ],
        out_specs=pl.BlockSpec((None, FFT_HEADS, SEQ, HEAD_DIM), lambda b, h: (b, h, 0, 0)),
        out_shape=jax.ShapeDtypeStruct((BATCH, N_HEADS, SEQ, HEAD_DIM), F32),
        scratch_shapes=[
            pltpu.VMEM((SEQ, 2 * FFT_W), BF16),
            pltpu.VMEM((FFT_Q, 2 * FFT_P, FFT_W), BF16),
        ],
        compiler_params=_params(("parallel", "parallel")),
        name="fourier",
    )(z3, cs, m2)


def _gelu(x):
    return 0.5 * x * (1.0 + lax.erf(x * np.float32(np.sqrt(0.5))))


def _mixer_body(x_ref, f_ref, zu_ref, zv_ref, zb_ref, zc_ref, zval_ref, zp_ref,
                zc_prev, zval_prev, zp_prev, zc_next, zval_next, zp_next,
                wf_ref, ws_ref, bs_ref, cw_ref, wp_ref, ps_ref, on_ref, wo_ref,
                o_ref, ext_a, ext_p, ybf):
    i = pl.program_id(1)
    ts = x_ref.shape[0]
    keep_prev = (i > 0).astype(F32)
    keep_next = (i < pl.num_programs(1) - 1).astype(F32)

    def norm_store(y, g):
        gs = slice(g * D_GROUP, (g + 1) * D_GROUP)
        ms = jnp.mean(y * y, axis=-1, keepdims=True)
        ybf[:, gs] = ((y * lax.rsqrt(ms + EPS)) * on_ref[:, gs]).astype(BF16)

    spec = jnp.concatenate([f_ref[h] for h in range(N_HEADS)], axis=1).astype(BF16)
    norm_store(jnp.dot(spec, wf_ref[...], preferred_element_type=F32), 0)

    gu = _gelu(zu_ref[...])
    gv = _gelu(zv_ref[...]).astype(BF16)
    cols = []
    for h in range(N_HEADS):
        hs = slice(h * HEAD_DIM, (h + 1) * HEAD_DIM)
        rows = []
        for n in range(ts // CHUNK):
            vv = gv[n * CHUNK:(n + 1) * CHUNK, hs]
            rows.append(jnp.dot(ws_ref[h], vv, preferred_element_type=F32) + bs_ref[h])
        cols.append(jnp.concatenate(rows, axis=0))
    norm_store(gu * jnp.concatenate(cols, axis=1), 1)

    a = zc_ref[...] * zval_ref[...]
    ext_a[0:HALO, :] = zc_prev[...] * zval_prev[...] * keep_prev
    ext_a[HALO:HALO + ts, :] = a
    ext_a[HALO + ts:2 * HALO + ts, :] = zc_next[...] * zval_next[...] * keep_next
    conv = (cw_ref[0:1, :] * ext_a[HALO - 1:HALO - 1 + ts, :] + cw_ref[1:2, :] * a
            + cw_ref[2:3, :] * ext_a[HALO + 1:HALO + 1 + ts, :])
    norm_store(zb_ref[...] * conv, 2)

    zp = zp_ref[...]
    ext_p[0:HALO, :] = zp_prev[...] * keep_prev
    ext_p[HALO:HALO + ts, :] = zp
    ext_p[HALO + ts:2 * HALO + ts, :] = zp_next[...] * keep_next
    t = i * ts + lax.broadcasted_iota(jnp.int32, (ts, HEAD_DIM), 0)
    outs = []
    for g, w in enumerate(POOL_WINDOWS):
        gs = slice(g * HEAD_DIM, (g + 1) * HEAD_DIM)
        acc = ext_p[HALO - w // 2:HALO - w // 2 + ts, gs]
        for d in range(-w // 2 + 1, w // 2):
            acc = acc + ext_p[HALO + d:HALO + d + ts, gs]
        cnt = (jnp.minimum(t + w // 2, SEQ) - jnp.maximum(t - w // 2, 0)).astype(F32)
        pg = acc / cnt - zp[:, gs]
        outs.append(jnp.dot(pg.astype(BF16), wp_ref[g], preferred_element_type=F32))
    norm_store(jnp.concatenate(outs, axis=1) * ps_ref[...], 3)

    o_ref[...] = x_ref[...] + jnp.dot(ybf[...], wo_ref[...], preferred_element_type=F32)


def _mixer(l, x3, f3, z3, wf, ws, bsb, cw, wp, ps, on, wo):
    ts = 512
    nb8 = ts // HALO
    last8 = SEQ // HALO - 1

    def col(j):
        return pl.BlockSpec((None, ts, D_GROUP), lambda b, i, j=j: (b, i, j))

    def prev(j):
        return pl.BlockSpec((None, HALO, D_GROUP),
                            lambda b, i, j=j: (b, jnp.maximum(i * nb8 - 1, 0), j))

    def nxt(j):
        return pl.BlockSpec((None, HALO, D_GROUP),
                            lambda b, i, j=j: (b, jnp.minimum((i + 1) * nb8, last8), j))

    def full(shape, **kw):
        return pl.BlockSpec((None,) + shape, lambda b, i: (l,) + (0,) * len(shape), **kw)

    return pl.pallas_call(
        _mixer_body,
        grid=(BATCH, SEQ // ts),
        in_specs=[
            pl.BlockSpec((None, ts, D_MODEL), lambda b, i: (b, i, 0)),
            pl.BlockSpec((None, N_HEADS, ts, HEAD_DIM), lambda b, i: (b, 0, i, 0)),
            col(1), col(2), col(3), col(4), col(5), col(6),
            prev(4), prev(5), prev(6), nxt(4), nxt(5), nxt(6),
            full((D_GROUP, D_GROUP)), full((N_HEADS, CHUNK, CHUNK)),
            full((N_HEADS, CHUNK, HEAD_DIM)), full((3, D_GROUP)),
            full((4, HEAD_DIM, HEAD_DIM)), full((1, D_GROUP)), full((1, D_MODEL)),
            full((D_MODEL, D_MODEL), pipeline_mode=pl.Buffered(1)),
        ],
        out_specs=pl.BlockSpec((None, ts, D_MODEL), lambda b, i: (b, i, 0)),
        out_shape=jax.ShapeDtypeStruct((BATCH, SEQ, D_MODEL), F32),
        scratch_shapes=[
            pltpu.VMEM((ts + 2 * HALO, D_GROUP), F32),
            pltpu.VMEM((ts + 2 * HALO, D_GROUP), F32),
            pltpu.VMEM((ts, D_MODEL), BF16),
        ],
        compiler_params=_params(("parallel", "arbitrary")),
        name="mixer",
    )(x3, f3, z3, z3, z3, z3, z3, z3, z3, z3, z3, z3, z3, z3, wf, ws, bsb, cw, wp, ps, on, wo)


def _router_body(x_ref, g_ref, wr_ref, wrl_ref, br_ref, tri_ref, hp_ref, ri_ref, rg_ref, cnt_ref,
                 carry):
    @pl.when(pl.program_id(0) == 0)
    def _():
        carry[...] = jnp.zeros_like(carry)

    x = x_ref[...]
    tr = x.shape[0]
    ms = jnp.mean(x * x, axis=-1, keepdims=True)
    h = (x * lax.rsqrt(ms + EPS)) * g_ref[...]

    _store_row_tiles(hp_ref, _pack_pairs(h))

    h_hi = h.astype(BF16)
    h_lo = (h - h_hi.astype(F32)).astype(BF16)
    logits = (jnp.dot(h_hi, wr_ref[...], preferred_element_type=F32)
              + jnp.dot(h_lo, wr_ref[...], preferred_element_type=F32)
              + jnp.dot(h_hi, wrl_ref[...], preferred_element_type=F32)) + br_ref[...]
    lane = lax.broadcasted_iota(jnp.int32, (tr, LANES), 1)
    neg = np.float32(-np.inf)

    def first_argmax(v, vmax):
        return jnp.min(jnp.where(v == vmax, lane, LANES), axis=-1, keepdims=True)

    mg = jnp.where(lane < N_EXPERT_GROUPS, logits, neg)
    gmax = jnp.max(mg, axis=-1, keepdims=True)
    grp = first_argmax(mg, gmax)
    p_grp = 1.0 / jnp.sum(jnp.exp(mg - gmax), axis=-1, keepdims=True)

    lo = ROUTE_OFF + EXPERTS_PER_GROUP * grp
    le = jnp.where(lane >= lo, jnp.where(lane < lo + EXPERTS_PER_GROUP, logits, neg), neg)
    emax = jnp.max(le, axis=-1, keepdims=True)
    i1 = first_argmax(le, emax)
    le2 = jnp.where(lane == i1, neg, le)
    emax2 = jnp.max(le2, axis=-1, keepdims=True)
    i2 = first_argmax(le2, emax2)
    e2 = jnp.exp(emax2 - emax)
    gate1 = p_grp / (1.0 + e2)
    gate2 = p_grp * e2 / (1.0 + e2)

    is1, is2 = lane == i1, lane == i2
    oh = jnp.where(is1, 1.0, jnp.where(is2, 1.0, 0.0))
    pref = jnp.dot(tri_ref[...], oh.astype(BF16), preferred_element_type=F32) + carry[...]
    r1 = jnp.sum(jnp.where(is1, pref, 0.0), axis=-1, keepdims=True).astype(jnp.int32)
    r2 = jnp.sum(jnp.where(is2, pref, 0.0), axis=-1, keepdims=True).astype(jnp.int32)
    carry[...] = carry[...] + jnp.sum(oh, axis=0, keepdims=True)
    cnt_ref[...] = carry[...]

    ri_ref[...] = jnp.where(lane == 0, i1 - ROUTE_OFF,
                            jnp.where(lane == 1, i2 - ROUTE_OFF,
                                      jnp.where(lane == 2, r1, jnp.where(lane == 3, r2, 0))))
    rg_ref[...] = jnp.where(lane == 0, gate1, jnp.where(lane == 1, gate2, 0.0))


def _router(l, x, g, wr, wrl, br, tri):
    tr = tri.shape[0]
    return pl.pallas_call(
        _router_body,
        grid=(N_TOK // tr,),
        in_specs=[
            pl.BlockSpec((tr, D_MODEL), lambda i: (i, 0)),
            pl.BlockSpec((None, 1, D_MODEL), lambda i: (l, 0, 0)),
            pl.BlockSpec((None, D_MODEL, LANES), lambda i: (l, 0, 0)),
            pl.BlockSpec((None, D_MODEL, LANES), lambda i: (l, 0, 0)),
            pl.BlockSpec((None, 1, LANES), lambda i: (l, 0, 0)),
            pl.BlockSpec((tr, tr), lambda i: (0, 0)),
        ],
        out_specs=[
            pl.BlockSpec((tr * ROW_TILE, LANES), lambda i: (i, 0)),
            pl.BlockSpec((tr, LANES), lambda i: (i, 0)),
            pl.BlockSpec((tr, LANES), lambda i: (i, 0)),
            pl.BlockSpec((1, LANES), lambda i: (0, 0)),
        ],
        out_shape=[
            jax.ShapeDtypeStruct((N_TOK * ROW_TILE, LANES), jnp.uint32),
            jax.ShapeDtypeStruct((N_TOK, LANES), jnp.int32),
            jax.ShapeDtypeStruct((N_TOK, LANES), F32),
            jax.ShapeDtypeStruct((1, LANES), F32),
        ],
        scratch_shapes=[pltpu.VMEM((1, LANES), F32)],
        compiler_params=_params(("arbitrary",)),
        name="router",
    )(x, g, wr, wrl, br, tri)


def _moe_body(be_ref, na_ref, nv_ref, gs_ref, sd_ref, hp_ref, wg32_ref, wu32_ref, wd32_ref, yk_ref,
              wg_ref, wu_ref, wd_ref, xbuf, ybuf, yacc, gsem, ssem):
    i = pl.program_id(0)
    na = na_ref[0]
    slot = lax.rem(i, 2)
    other = 1 - slot
    blk_rows = TM_MOE * ROW_TILE

    def tile_rows(r):
        start = r * ROW_TILE
        return pl.ds(start if isinstance(r, int) else pl.multiple_of(start, ROW_TILE), ROW_TILE)

    def gather_copy(blk, s, r):
        src = pl.multiple_of(gs_ref[blk * TM_MOE + r], ROW_TILE)
        return pltpu.make_async_copy(hp_ref.at[pl.ds(src, ROW_TILE), :],
                                     xbuf.at[s, tile_rows(r), :], gsem.at[s])

    def scatter_copy(blk, s, r):
        dst = pl.multiple_of(sd_ref[blk * TM_MOE + r], ROW_TILE)
        return pltpu.make_async_copy(ybuf.at[s, tile_rows(r), :],
                                     yk_ref.at[pl.ds(dst, ROW_TILE), :], ssem.at[s])

    def wait_gather(s):
        pltpu.make_async_copy(hp_ref.at[pl.ds(0, blk_rows), :], xbuf.at[s], gsem.at[s]).wait()

    def wait_scatter(blk, s):
        n = nv_ref[blk]

        @pl.when(n == TM_MOE)
        def _():
            pltpu.make_async_copy(ybuf.at[s], yk_ref.at[pl.ds(0, blk_rows), :], ssem.at[s]).wait()

        @pl.when(n < TM_MOE)
        def _():
            lax.fori_loop(0, n, lambda r, c: (scatter_copy(blk, s, r).wait(), c)[1], 0)

    yslot = lax.rem(i, 3)
    yprev = lax.rem(i + 2, 3)
    nv_prev = nv_ref[jnp.maximum(i - 1, 0)]
    prev_full = (i >= 1) & (nv_prev == TM_MOE)

    def start(copy):
        copy.start(priority=ROW_DMA_PRIORITY)

    @pl.when(i == 0)
    def _():
        lax.fori_loop(0, TM_MOE, lambda r, c: (start(gather_copy(0, 0, r)), c)[1], 0)

    @pl.when((i < na) & ((i == 0) | (be_ref[i] != be_ref[jnp.maximum(i - 1, 0)])))
    def _():
        wg_ref[...] = wg32_ref[...].astype(BF16)
        wu_ref[...] = wu32_ref[...].astype(BF16)
        wd_ref[...] = wd32_ref[...].astype(BF16)

    @pl.when((i >= 3) & (i < na))
    def _():
        wait_scatter(i - 3, yslot)

    @pl.when((i >= 1) & (i < na) & (nv_prev < TM_MOE))
    def _():
        lax.fori_loop(0, nv_prev, lambda r, c: (start(scatter_copy(i - 1, yprev, r)), c)[1], 0)

    def block_body(scatter_prev):
        wait_gather(slot)
        nxt = jnp.minimum(i + 1, na - 1)
        for r in range(TM_MOE):
            start(gather_copy(nxt, other, r))
            if scatter_prev:
                start(scatter_copy(i - 1, yprev, r))
        xu = _load_row_tiles(xbuf.at[slot], TM_MOE)
        lo = pltpu.bitcast(xu << 16, F32).astype(BF16)
        hi = pltpu.bitcast(xu & jnp.uint32(0xFFFF0000), F32).astype(BF16)
        fc = 256
        for c in range(D_EXPERT // fc):
            cs = slice(c * fc, (c + 1) * fc)
            g = (jnp.dot(lo, wg_ref[0:D_HALF, cs], preferred_element_type=F32)
                 + jnp.dot(hi, wg_ref[D_HALF:D_MODEL, cs], preferred_element_type=F32))
            u = (jnp.dot(lo, wu_ref[0:D_HALF, cs], preferred_element_type=F32)
                 + jnp.dot(hi, wu_ref[D_HALF:D_MODEL, cs], preferred_element_type=F32))
            a = (g * jax.nn.sigmoid(g) * u).astype(BF16)
            part = jnp.dot(a, wd_ref[cs, :], preferred_element_type=F32)
            if c == 0:
                yacc[...] = part
            else:
                yacc[...] += part
        _store_row_tiles(ybuf.at[yslot], _pack_pairs(yacc[...]))

    @pl.when((i < na) & prev_full)
    def _():
        block_body(True)

    @pl.when((i < na) & jnp.logical_not(prev_full))
    def _():
        block_body(False)

    @pl.when(i == na - 1)
    def _():
        lax.fori_loop(0, nv_ref[i], lambda r, c: (start(scatter_copy(i, yslot, r)), c)[1], 0)
        wait_gather(other)
        wait_scatter(i, yslot)

        @pl.when(i >= 1)
        def _():
            wait_scatter(i - 1, yprev)

        @pl.when(i >= 2)
        def _():
            wait_scatter(i - 2, lax.rem(i + 1, 3))


def _moe(l, blk_exp, n_active, n_valid, gsrc, sdst, hp, wg, wu, wd):
    def w_blk(i, be, na, nv, gs, sd):
        return (l, be[i], 0, 0)

    return pl.pallas_call(
        _moe_body,
        grid_spec=pltpu.PrefetchScalarGridSpec(
            num_scalar_prefetch=5,
            grid=(N_BLOCKS,),
            in_specs=[
                pl.BlockSpec(memory_space=pl.ANY),
                pl.BlockSpec((None, None, D_MODEL, D_EXPERT), w_blk),
                pl.BlockSpec((None, None, D_MODEL, D_EXPERT), w_blk),
                pl.BlockSpec((None, None, D_EXPERT, D_MODEL), w_blk),
            ],
            out_specs=pl.BlockSpec(memory_space=pl.ANY),
            scratch_shapes=[
                pltpu.VMEM((D_MODEL, D_EXPERT), BF16),
                pltpu.VMEM((D_MODEL, D_EXPERT), BF16),
                pltpu.VMEM((D_EXPERT, D_MODEL), BF16),
                pltpu.VMEM((2, TM_MOE * ROW_TILE, LANES), jnp.uint32),
                pltpu.VMEM((3, TM_MOE * ROW_TILE, LANES), jnp.uint32),
                pltpu.VMEM((TM_MOE, D_MODEL), F32),
                pltpu.SemaphoreType.DMA((2,)),
                pltpu.SemaphoreType.DMA((3,)),
            ],
        ),
        out_shape=jax.ShapeDtypeStruct((2 * N_TOK * ROW_TILE, LANES), jnp.uint32),
        compiler_params=_params(("arbitrary",), vmem=60 * 1024 * 1024),
        name="moe",
    )(blk_exp, n_active, n_valid, gsrc, sdst, hp, wg, wu, wd)


def _final_body(x_ref, y0_ref, y1_ref, rg_ref, g_ref, o_ref):
    y = _moe_residual(x_ref, y0_ref, y1_ref, rg_ref)
    ms = jnp.mean(y * y, axis=-1, keepdims=True)
    o_ref[...] = (y * lax.rsqrt(ms + EPS)) * g_ref[...]


def _final(x, yk, rg, g):
    tm = 512
    row = pl.BlockSpec((tm, D_MODEL), lambda i: (i, 0))
    return pl.pallas_call(
        _final_body,
        grid=(N_TOK // tm,),
        in_specs=[
            row,
            pl.BlockSpec((None, tm * ROW_TILE, LANES), lambda i: (0, i, 0)),
            pl.BlockSpec((None, tm * ROW_TILE, LANES), lambda i: (1, i, 0)),
            pl.BlockSpec((tm, LANES), lambda i: (i, 0)),
            pl.BlockSpec((1, D_MODEL), lambda i: (0, 0)),
        ],
        out_specs=row,
        out_shape=jax.ShapeDtypeStruct((N_TOK, D_MODEL), F32),
        compiler_params=_params(("parallel",)),
        name="final",
    )(x, yk, yk, rg, g)


def _plan(ri, cnt):
    counts = cnt[0, ROUTE_OFF:ROUTE_OFF + N_EXPERTS].astype(jnp.int32)
    nblk = (counts + TM_MOE - 1) // TM_MOE
    blk_end = jnp.cumsum(nblk)
    blk_start = blk_end - nblk
    pad_start = blk_start * TM_MOE
    n_active = blk_end[-1:]
    blk = jnp.minimum(jnp.arange(N_BLOCKS, dtype=jnp.int32), n_active[0] - 1)
    blk_exp = jnp.sum(blk_end[None, :] <= blk[:, None], axis=1)
    n_valid = jnp.clip(counts[blk_exp] - (blk - blk_start[blk_exp]) * TM_MOE, 0, TM_MOE)
    eid, rank = ri[:, 0:2], ri[:, 2:4]
    onehot = eid[:, :, None] == jnp.arange(N_EXPERTS, dtype=jnp.int32)[None, None, :]
    dest = rank + jnp.sum(jnp.where(onehot, pad_start[None, None, :], 0), axis=-1)
    row_asg = jnp.zeros((N_ROWS,), jnp.int32).at[dest.reshape(-1)].set(
        jnp.arange(2 * N_TOK, dtype=jnp.int32), unique_indices=True)
    tok, k = row_asg >> 1, row_asg & 1
    gsrc = tok * ROW_TILE
    sdst = (k * N_TOK + tok) * ROW_TILE
    return (blk_exp.astype(jnp.int32), n_active.astype(jnp.int32), n_valid.astype(jnp.int32),
            gsrc, sdst)


def kernel(x, mix_norm, w_in, w_fourier, w_spatial, b_spatial, conv_w, w_pool, pool_scale,
           out_norm, w_out, ffn_norm, w_group, b_group, w_router, b_router, w_gate, w_up,
           w_down, final_norm):
    cs, m2 = _fourier_consts()
    tr = 512
    tri = (jnp.arange(tr)[:, None] > jnp.arange(tr)[None, :]).astype(BF16)

    w_in_b = w_in.astype(BF16)
    w_f_b = w_fourier.astype(BF16)
    w_s_b = w_spatial.astype(BF16)
    w_p_b = w_pool.astype(BF16)
    w_o_b = w_out.astype(BF16)
    pad = LANES - N_EXPERT_GROUPS - N_EXPERTS
    wr = jnp.concatenate([w_group, w_router, jnp.zeros((DEPTH, D_MODEL, pad), F32)], axis=-1)
    wr_hi = wr.astype(BF16)
    wr_lo = (wr - wr_hi.astype(F32)).astype(BF16)
    br = jnp.concatenate([b_group, b_router, jnp.zeros((DEPTH, pad), F32)], axis=-1)[:, None, :]
    bsb = jnp.broadcast_to(b_spatial[:, :, :, None], (DEPTH, N_HEADS, CHUNK, HEAD_DIM))
    mix_g = mix_norm[:, None, :]
    ffn_g = ffn_norm[:, None, :]
    out_g = out_norm.reshape(DEPTH, 1, D_MODEL)
    pool_s = pool_scale[:, None, :]

    xt = x.reshape(N_TOK, D_MODEL)
    moe = None
    for l in range(DEPTH):
        if moe is None:
            z = _inproj(l, xt, mix_g, w_in_b)
        else:
            z, xt = _inproj(l, xt, mix_g, w_in_b, moe=moe)
        z3 = z.reshape(BATCH, SEQ, D_IN_PROJ)
        f3 = _fourier(z3, cs, m2)
        x3 = _mixer(l, xt.reshape(BATCH, SEQ, D_MODEL), f3, z3, w_f_b, w_s_b, bsb,
                    conv_w, w_p_b, pool_s, out_g, w_o_b)
        xt = x3.reshape(N_TOK, D_MODEL)
        hp, ri, rg, cnt = _router(l, xt, ffn_g, wr_hi, wr_lo, br, tri)
        blk_exp, n_active, n_valid, gsrc, sdst = _plan(ri, cnt)
        yk = _moe(l, blk_exp, n_active, n_valid, gsrc, sdst, hp, w_gate, w_up, w_down)
        moe = (yk.reshape(2, N_TOK * ROW_TILE, LANES), rg)
    xt = _final(xt, moe[0], moe[1], final_norm[None, :])
    return xt.reshape(BATCH, SEQ, D_MODEL)
```

```python
import functools

import numpy as np
import jax
import jax.numpy as jnp
from jax import lax
from jax.experimental import pallas as pl
from jax.experimental.pallas import tpu as pltpu

D_MODEL = 2048
BATCH = 4
SEQ = 4096
DEPTH = 4
N_TOK = BATCH * SEQ
D_GROUP = 512
HEAD_DIM = 128
N_HEADS = 4
CHUNK = 128
POOL_WINDOWS = (2, 4, 8, 16)
D_IN_PROJ = 7 * D_GROUP
N_EXPERT_GROUPS = 4
EXPERTS_PER_GROUP = 8
N_EXPERTS = 32
D_EXPERT = 768
EPS = 1e-6

F32 = jnp.float32
BF16 = jnp.bfloat16

FFT_Q = 8
FFT_P = SEQ // FFT_Q
FFT_HEADS = 2
FFT_W = FFT_HEADS * HEAD_DIM
HALO = 8
LANES = 128
ROUTE_OFF = N_EXPERT_GROUPS

TM_MOE = 256
N_BLOCKS = (2 * N_TOK) // TM_MOE + N_EXPERTS
N_ROWS = N_BLOCKS * TM_MOE
D_HALF = D_MODEL // 2
ROW_TILE = D_HALF // LANES
ROW_DMA_PRIORITY = 1

VMEM_LIMIT = 56 * 1024 * 1024


def _params(sem, vmem=VMEM_LIMIT):
    return pltpu.CompilerParams(dimension_semantics=sem, vmem_limit_bytes=vmem)


def _unpack_pairs(u):
    lo = pltpu.bitcast(u << 16, F32)
    hi = pltpu.bitcast(u & jnp.uint32(0xFFFF0000), F32)
    return jnp.concatenate([lo, hi], axis=1)


def _pack_pairs(v):
    bits = pltpu.bitcast(v.astype(BF16).astype(F32), jnp.uint32)
    return (bits[:, :D_HALF] >> 16) | (bits[:, D_HALF:] & jnp.uint32(0xFFFF0000))


def _store_row_tiles(ref, packed):
    rows = packed.shape[0]
    for s in range(ROW_TILE):
        ref[pl.ds(s, rows, stride=ROW_TILE), :] = packed[:, s * LANES:(s + 1) * LANES]


def _load_row_tiles(ref, rows):
    return jnp.concatenate(
        [ref[pl.ds(s, rows, stride=ROW_TILE), :] for s in range(ROW_TILE)], axis=1)


def _moe_residual(x_ref, y0_ref, y1_ref, rg_ref):
    rg = rg_ref[...]
    rows = x_ref.shape[0]
    return (x_ref[...] + rg[:, 0:1] * _unpack_pairs(_load_row_tiles(y0_ref, rows))
            + rg[:, 1:2] * _unpack_pairs(_load_row_tiles(y1_ref, rows)))


def _project(x, g_ref, w_ref, z_ref):
    ms = jnp.mean(x * x, axis=-1, keepdims=True)
    h = ((x * lax.rsqrt(ms + EPS)) * g_ref[...]).astype(BF16)
    tn = 512
    for j in range(D_IN_PROJ // tn):
        cs = slice(j * tn, (j + 1) * tn)
        z_ref[:, cs] = jnp.dot(h, w_ref[:, cs], preferred_element_type=F32)


def _inproj_body(x_ref, g_ref, w_ref, z_ref):
    _project(x_ref[...], g_ref, w_ref, z_ref)


def _inproj_moe_body(x_ref, y0_ref, y1_ref, rg_ref, g_ref, w_ref, z_ref, xo_ref):
    x = _moe_residual(x_ref, y0_ref, y1_ref, rg_ref)
    xo_ref[...] = x
    _project(x, g_ref, w_ref, z_ref)


def _inproj(l, x, g, w, moe=None):
    tm = 512 if moe is None else 256
    row = pl.BlockSpec((tm, D_MODEL), lambda i: (i, 0))
    w_specs = [
        pl.BlockSpec((None, 1, D_MODEL), lambda i: (l, 0, 0)),
        pl.BlockSpec((None, D_MODEL, D_IN_PROJ), lambda i: (l, 0, 0), pipeline_mode=pl.Buffered(1)),
    ]
    z_spec = pl.BlockSpec((tm, D_IN_PROJ), lambda i: (i, 0))
    z_shape = jax.ShapeDtypeStruct((N_TOK, D_IN_PROJ), F32)
    if moe is None:
        return pl.pallas_call(
            _inproj_body, grid=(N_TOK // tm,), in_specs=[row] + w_specs, out_specs=z_spec,
            out_shape=z_shape, compiler_params=_params(("parallel",)), name="inproj",
        )(x, g, w)
    yk, rg = moe
    return pl.pallas_call(
        _inproj_moe_body,
        grid=(N_TOK // tm,),
        in_specs=[
            row,
            pl.BlockSpec((None, tm * ROW_TILE, LANES), lambda i: (0, i, 0)),
            pl.BlockSpec((None, tm * ROW_TILE, LANES), lambda i: (1, i, 0)),
            pl.BlockSpec((tm, LANES), lambda i: (i, 0)),
        ] + w_specs,
        out_specs=[z_spec, row],
        out_shape=[z_shape, jax.ShapeDtypeStruct((N_TOK, D_MODEL), F32)],
        compiler_params=_params(("parallel",)),
        name="inproj_moe",
    )(x, yk, yk, rg, g, w)


def _fourier_consts():
    c = np.arange(HEAD_DIM)
    ang = 2.0 * np.pi * (np.outer(c, c) % HEAD_DIM) / HEAD_DIM
    cc, sc = np.cos(ang), np.sin(ang)
    scale = 1.0 / np.sqrt(SEQ * HEAD_DIM)
    cs = np.block([[cc, -sc], [sc, cc]]) * scale
    s1 = np.arange(FFT_P)
    m2 = np.zeros((FFT_Q, FFT_P, 2 * FFT_P), np.float64)
    for k2 in range(FFT_Q):
        k = FFT_Q * np.arange(FFT_P) + k2
        th = 2.0 * np.pi * (np.outer(k, s1) % SEQ) / SEQ
        m2[k2, :, :FFT_P] = np.cos(th)
        m2[k2, :, FFT_P:] = np.sin(th)
    return jnp.asarray(cs, BF16), jnp.asarray(m2, BF16)


def _fourier_body(z_ref, cs_ref, m2_ref, f_ref, y_scr, u_scr):
    rc = 64
    r = np.float32(np.sqrt(0.5))

    def chunk(ci, carry):
        r0 = pl.multiple_of(ci * rc, rc)
        zb = [z_ref[pl.ds(s2 * FFT_P + r0, rc), :] for s2 in range(FFT_Q)]
        e0, e1 = zb[0] + zb[4], zb[0] - zb[4]
        e2, e3 = zb[2] + zb[6], zb[2] - zb[6]
        o0, o1 = zb[1] + zb[5], zb[1] - zb[5]
        o2, o3 = zb[3] + zb[7], zb[3] - zb[7]
        p, q = (o1 - o3) * r, (o1 + o3) * r
        ee, oo = e0 + e2, o0 + o2
        ed, od = e0 - e2, o0 - o2
        zero = jnp.zeros_like(e0)
        ys = [(ee + oo, zero), (e1 + p, -e3 - q), (ed, -od), (e1 - p, e3 - q),
              (ee - oo, zero), (e1 - p, q - e3), (ed, od), (e1 + p, e3 + q)]
        for k2 in range(FFT_Q):
            re, im = ys[k2]
            rows = pl.ds(k2 * FFT_P + r0, rc)
            for h in range(FFT_HEADS):
                hs = slice(h * HEAD_DIM, (h + 1) * HEAD_DIM)
                y_scr[rows, 2 * h * HEAD_DIM:(2 * h + 1) * HEAD_DIM] = re[:, hs].astype(BF16)
                y_scr[rows, (2 * h + 1) * HEAD_DIM:(2 * h + 2) * HEAD_DIM] = im[:, hs].astype(BF16)
        return carry

    lax.fori_loop(0, FFT_P // rc, chunk, 0)

    for k2 in range(FFT_Q):
        for h in range(FFT_HEADS):
            yk = y_scr[k2 * FFT_P:(k2 + 1) * FFT_P, 2 * h * HEAD_DIM:(2 * h + 2) * HEAD_DIM]
            ab = jnp.dot(yk, cs_ref[...], preferred_element_type=F32)
            hs = slice(h * HEAD_DIM, (h + 1) * HEAD_DIM)
            u_scr[k2, 0:FFT_P, hs] = ab[:, :HEAD_DIM].astype(BF16)
            u_scr[k2, FFT_P:2 * FFT_P, hs] = ab[:, HEAD_DIM:].astype(BF16)

    for k2 in range(FFT_Q):
        res = jnp.dot(m2_ref[k2], u_scr[k2], preferred_element_type=F32)
        for h in range(FFT_HEADS):
            f_ref[h, pl.ds(k2, FFT_P, stride=FFT_Q), :] = res[:, h * HEAD_DIM:(h + 1) * HEAD_DIM]


def _fourier(z3, cs, m2):
    return pl.pallas_call(
        _fourier_body,
        grid=(BATCH, N_HEADS // FFT_HEADS),
        in_specs=[
            pl.BlockSpec((None, SEQ, FFT_W), lambda b, h: (b, 0, h)),
            pl.BlockSpec((2 * HEAD_DIM, 2 * HEAD_DIM), lambda b, h: (0, 0)),
            pl.BlockSpec((FFT_Q, FFT_P, 2 * FFT_P), lambda b, h: (0, 0, 0)),
        ],
        out_specs=pl.BlockSpec((None, FFT_HEADS, SEQ, HEAD_DIM), lambda b, h: (b, h, 0, 0)),
        out_shape=jax.ShapeDtypeStruct((BATCH, N_HEADS, SEQ, HEAD_DIM), F32),
        scratch_shapes=[
            pltpu.VMEM((SEQ, 2 * FFT_W), BF16),
            pltpu.VMEM((FFT_Q, 2 * FFT_P, FFT_W), BF16),
        ],
        compiler_params=_params(("parallel", "parallel")),
        name="fourier",
    )(z3, cs, m2)


def _gelu(x):
    return 0.5 * x * (1.0 + lax.erf(x * np.float32(np.sqrt(0.5))))


def _mixer_body(x_ref, f_ref, zu_ref, zv_ref, zb_ref, zc_ref, zval_ref, zp_ref,
                zc_prev, zval_prev, zp_prev, zc_next, zval_next, zp_next,
                wf_ref, ws_ref, bs_ref, cw_ref, wp_ref, ps_ref, on_ref, wo_ref,
                o_ref, ext_a, ext_p, ybf):
    i = pl.program_id(1)
    ts = x_ref.shape[0]
    keep_prev = (i > 0).astype(F32)
    keep_next = (i < pl.num_programs(1) - 1).astype(F32)

    def norm_store(y, g):
        gs = slice(g * D_GROUP, (g + 1) * D_GROUP)
        ms = jnp.mean(y * y, axis=-1, keepdims=True)
        ybf[:, gs] = ((y * lax.rsqrt(ms + EPS)) * on_ref[:, gs]).astype(BF16)

    spec = jnp.concatenate([f_ref[h] for h in range(N_HEADS)], axis=1).astype(BF16)
    norm_store(jnp.dot(spec, wf_ref[...], preferred_element_type=F32), 0)

    gu = _gelu(zu_ref[...])
    gv = _gelu(zv_ref[...]).astype(BF16)
    cols = []
    for h in range(N_HEADS):
        hs = slice(h * HEAD_DIM, (h + 1) * HEAD_DIM)
        rows = []
        for n in range(ts // CHUNK):
            vv = gv[n * CHUNK:(n + 1) * CHUNK, hs]
            rows.append(jnp.dot(ws_ref[h], vv, preferred_element_type=F32) + bs_ref[h])
        cols.append(jnp.concatenate(rows, axis=0))
    norm_store(gu * jnp.concatenate(cols, axis=1), 1)

    a = zc_ref[...] * zval_ref[...]
    ext_a[0:HALO, :] = zc_prev[...] * zval_prev[...] * keep_prev
    ext_a[HALO:HALO + ts, :] = a
    ext_a[HALO + ts:2 * HALO + ts, :] = zc_next[...] * zval_next[...] * keep_next
    conv = (cw_ref[0:1, :] * ext_a[HALO - 1:HALO - 1 + ts, :] + cw_ref[1:2, :] * a
            + cw_ref[2:3, :] * ext_a[HALO + 1:HALO + 1 + ts, :])
    norm_store(zb_ref[...] * conv, 2)

    zp = zp_ref[...]
    ext_p[0:HALO, :] = zp_prev[...] * keep_prev
    ext_p[HALO:HALO + ts, :] = zp
    ext_p[HALO + ts:2 * HALO + ts, :] = zp_next[...] * keep_next
    t = i * ts + lax.broadcasted_iota(jnp.int32, (ts, HEAD_DIM), 0)
    outs = []
    for g, w in enumerate(POOL_WINDOWS):
        gs = slice(g * HEAD_DIM, (g + 1) * HEAD_DIM)
        acc = ext_p[HALO - w // 2:HALO - w // 2 + ts, gs]
        for d in range(-w // 2 + 1, w // 2):
            acc = acc + ext_p[HALO + d:HALO + d + ts, gs]
        cnt = (jnp.minimum(t + w // 2, SEQ) - jnp.maximum(t - w // 2, 0)).astype(F32)
        pg = acc / cnt - zp[:, gs]
        outs.append(jnp.dot(pg.astype(BF16), wp_ref[g], preferred_element_type=F32))
    norm_store(jnp.concatenate(outs, axis=1) * ps_ref[...], 3)

    o_ref[...] = x_ref[...] + jnp.dot(ybf[...], wo_ref[...], preferred_element_type=F32)


def _mixer(l, x3, f3, z3, wf, ws, bsb, cw, wp, ps, on, wo):
    ts = 512
    nb8 = ts // HALO
    last8 = SEQ // HALO - 1

    def col(j):
        return pl.BlockSpec((None, ts, D_GROUP), lambda b, i, j=j: (b, i, j))

    def prev(j):
        return pl.BlockSpec((None, HALO, D_GROUP),
                            lambda b, i, j=j: (b, jnp.maximum(i * nb8 - 1, 0), j))

    def nxt(j):
        return pl.BlockSpec((None, HALO, D_GROUP),
                            lambda b, i, j=j: (b, jnp.minimum((i + 1) * nb8, last8), j))

    def full(shape, **kw):
        return pl.BlockSpec((None,) + shape, lambda b, i: (l,) + (0,) * len(shape), **kw)

    return pl.pallas_call(
        _mixer_body,
        grid=(BATCH, SEQ // ts),
        in_specs=[
            pl.BlockSpec((None, ts, D_MODEL), lambda b, i: (b, i, 0)),
            pl.BlockSpec((None, N_HEADS, ts, HEAD_DIM), lambda b, i: (b, 0, i, 0)),
            col(1), col(2), col(3), col(4), col(5), col(6),
            prev(4), prev(5), prev(6), nxt(4), nxt(5), nxt(6),
            full((D_GROUP, D_GROUP)), full((N_HEADS, CHUNK, CHUNK)),
            full((N_HEADS, CHUNK, HEAD_DIM)), full((3, D_GROUP)),
            full((4, HEAD_DIM, HEAD_DIM)), full((1, D_GROUP)), full((1, D_MODEL)),
            full((D_MODEL, D_MODEL), pipeline_mode=pl.Buffered(1)),
        ],
        out_specs=pl.BlockSpec((None, ts, D_MODEL), lambda b, i: (b, i, 0)),
        out_shape=jax.ShapeDtypeStruct((BATCH, SEQ, D_MODEL), F32),
        scratch_shapes=[
            pltpu.VMEM((ts + 2 * HALO, D_GROUP), F32),
            pltpu.VMEM((ts + 2 * HALO, D_GROUP), F32),
            pltpu.VMEM((ts, D_MODEL), BF16),
        ],
        compiler_params=_params(("parallel", "arbitrary")),
        name="mixer",
    )(x3, f3, z3, z3, z3, z3, z3, z3, z3, z3, z3, z3, z3, z3, wf, ws, bsb, cw, wp, ps, on, wo)


def _router_body(x_ref, g_ref, wr_ref, wrl_ref, br_ref, tri_ref, hp_ref, ri_ref, rg_ref, cnt_ref,
                 carry):
    @pl.when(pl.program_id(0) == 0)
    def _():
        carry[...] = jnp.zeros_like(carry)

    x = x_ref[...]
    tr = x.shape[0]
    ms = jnp.mean(x * x, axis=-1, keepdims=True)
    h = (x * lax.rsqrt(ms + EPS)) * g_ref[...]

    _store_row_tiles(hp_ref, _pack_pairs(h))

    h_hi = h.astype(BF16)
    h_lo = (h - h_hi.astype(F32)).astype(BF16)
    logits = (jnp.dot(h_hi, wr_ref[...], preferred_element_type=F32)
              + jnp.dot(h_lo, wr_ref[...], preferred_element_type=F32)
              + jnp.dot(h_hi, wrl_ref[...], preferred_element_type=F32)) + br_ref[...]
    lane = lax.broadcasted_iota(jnp.int32, (tr, LANES), 1)
    neg = np.float32(-np.inf)

    def first_argmax(v, vmax):
        return jnp.min(jnp.where(v == vmax, lane, LANES), axis=-1, keepdims=True)

    mg = jnp.where(lane < N_EXPERT_GROUPS, logits, neg)
    gmax = jnp.max(mg, axis=-1, keepdims=True)
    grp = first_argmax(mg, gmax)
    p_grp = 1.0 / jnp.sum(jnp.exp(mg - gmax), axis=-1, keepdims=True)

    lo = ROUTE_OFF + EXPERTS_PER_GROUP * grp
    le = jnp.where(lane >= lo, jnp.where(lane < lo + EXPERTS_PER_GROUP, logits, neg), neg)
    emax = jnp.max(le, axis=-1, keepdims=True)
    i1 = first_argmax(le, emax)
    le2 = jnp.where(lane == i1, neg, le)
    emax2 = jnp.max(le2, axis=-1, keepdims=True)
    i2 = first_argmax(le2, emax2)
    e2 = jnp.exp(emax2 - emax)
    gate1 = p_grp / (1.0 + e2)
    gate2 = p_grp * e2 / (1.0 + e2)

    is1, is2 = lane == i1, lane == i2
    oh = jnp.where(is1, 1.0, jnp.where(is2, 1.0, 0.0))
    pref = jnp.dot(tri_ref[...], oh.astype(BF16), preferred_element_type=F32) + carry[...]
    r1 = jnp.sum(jnp.where(is1, pref, 0.0), axis=-1, keepdims=True).astype(jnp.int32)
    r2 = jnp.sum(jnp.where(is2, pref, 0.0), axis=-1, keepdims=True).astype(jnp.int32)
    carry[...] = carry[...] + jnp.sum(oh, axis=0, keepdims=True)
    cnt_ref[...] = carry[...]

    ri_ref[...] = jnp.where(lane == 0, i1 - ROUTE_OFF,
                            jnp.where(lane == 1, i2 - ROUTE_OFF,
                                      jnp.where(lane == 2, r1, jnp.where(lane == 3, r2, 0))))
    rg_ref[...] = jnp.where(lane == 0, gate1, jnp.where(lane == 1, gate2, 0.0))


def _router(l, x, g, wr, wrl, br, tri):
    tr = tri.shape[0]
    return pl.pallas_call(
        _router_body,
        grid=(N_TOK // tr,),
        in_specs=[
            pl.BlockSpec((tr, D_MODEL), lambda i: (i, 0)),
            pl.BlockSpec((None, 1, D_MODEL), lambda i: (l, 0, 0)),
            pl.BlockSpec((None, D_MODEL, LANES), lambda i: (l, 0, 0)),
            pl.BlockSpec((None, D_MODEL, LANES), lambda i: (l, 0, 0)),
            pl.BlockSpec((None, 1, LANES), lambda i: (l, 0, 0)),
            pl.BlockSpec((tr, tr), lambda i: (0, 0)),
        ],
        out_specs=[
            pl.BlockSpec((tr * ROW_TILE, LANES), lambda i: (i, 0)),
            pl.BlockSpec((tr, LANES), lambda i: (i, 0)),
            pl.BlockSpec((tr, LANES), lambda i: (i, 0)),
            pl.BlockSpec((1, LANES), lambda i: (0, 0)),
        ],
        out_shape=[
            jax.ShapeDtypeStruct((N_TOK * ROW_TILE, LANES), jnp.uint32),
            jax.ShapeDtypeStruct((N_TOK, LANES), jnp.int32),
            jax.ShapeDtypeStruct((N_TOK, LANES), F32),
            jax.ShapeDtypeStruct((1, LANES), F32),
        ],
        scratch_shapes=[pltpu.VMEM((1, LANES), F32)],
        compiler_params=_params(("arbitrary",)),
        name="router",
    )(x, g, wr, wrl, br, tri)


def _invert_body(dest_ref, pad_ref, rowv_ref):
    def fill(j, c):
        def zero(r, c2):
            rowv_ref[r] = 0
            return c2

        lax.fori_loop(pad_ref[2 * j], pad_ref[2 * j + 1], zero, 0)
        return c

    lax.fori_loop(0, N_EXPERTS + 1, fill, 0)

    unroll = 16

    def body(j, c):
        rows = [dest_ref[j * unroll + u] for u in range(unroll)]
        for u in range(unroll):
            rowv_ref[rows[u]] = j * unroll + u
        return c

    lax.fori_loop(0, 2 * N_TOK // unroll, body, 0)


def _invert(dest, pad_ranges):
    return pl.pallas_call(
        _invert_body,
        grid_spec=pltpu.PrefetchScalarGridSpec(
            num_scalar_prefetch=2,
            grid=(1,),
            in_specs=[],
            out_specs=pl.BlockSpec(memory_space=pltpu.SMEM),
        ),
        out_shape=jax.ShapeDtypeStruct((N_ROWS,), jnp.int32),
        compiler_params=_params(("arbitrary",)),
        name="invert",
    )(dest, pad_ranges)


def _moe_body(be_ref, na_ref, nv_ref, gs_ref, sd_ref, hp_ref, wg32_ref, wu32_ref, wd32_ref, yk_ref,
              wg_ref, wu_ref, wd_ref, xbuf, ybuf, yacc, gsem, ssem):
    i = pl.program_id(0)
    na = na_ref[0]
    slot = lax.rem(i, 2)
    other = 1 - slot
    blk_rows = TM_MOE * ROW_TILE

    def tile_rows(r):
        start = r * ROW_TILE
        return pl.ds(start if isinstance(r, int) else pl.multiple_of(start, ROW_TILE), ROW_TILE)

    def gather_copy(blk, s, r, src=None):
        if src is None:
            src = gs_ref[blk * TM_MOE + r]
        return pltpu.make_async_copy(hp_ref.at[pl.ds(pl.multiple_of(src, ROW_TILE), ROW_TILE), :],
                                     xbuf.at[s, tile_rows(r), :], gsem.at[s])

    def scatter_copy(blk, s, r, dst=None):
        if dst is None:
            dst = sd_ref[blk * TM_MOE + r]
        return pltpu.make_async_copy(ybuf.at[s, tile_rows(r), :],
                                     yk_ref.at[pl.ds(pl.multiple_of(dst, ROW_TILE), ROW_TILE), :],
                                     ssem.at[s])

    def wait_gather(s):
        pltpu.make_async_copy(hp_ref.at[pl.ds(0, blk_rows), :], xbuf.at[s], gsem.at[s]).wait()

    def wait_scatter(blk, s):
        n = nv_ref[blk]

        @pl.when(n == TM_MOE)
        def _():
            pltpu.make_async_copy(ybuf.at[s], yk_ref.at[pl.ds(0, blk_rows), :], ssem.at[s]).wait()

        @pl.when(n < TM_MOE)
        def _():
            lax.fori_loop(0, n, lambda r, c: (scatter_copy(blk, s, r).wait(), c)[1], 0)

    yslot = lax.rem(i, 3)
    yprev = lax.rem(i + 2, 3)
    nv_prev = nv_ref[jnp.maximum(i - 1, 0)]
    prev_full = (i >= 1) & (nv_prev == TM_MOE)

    def start(copy, priority=ROW_DMA_PRIORITY):
        copy.start(priority=priority)

    @pl.when(i == 0)
    def _():
        lax.fori_loop(0, TM_MOE, lambda r, c: (start(gather_copy(0, 0, r)), c)[1], 0)

    @pl.when((i < na) & ((i == 0) | (be_ref[i] != be_ref[jnp.maximum(i - 1, 0)])))
    def _():
        wg_ref[...] = wg32_ref[...].astype(BF16)
        wu_ref[...] = wu32_ref[...].astype(BF16)
        wd_ref[...] = wd32_ref[...].astype(BF16)

    @pl.when((i >= 3) & (i < na))
    def _():
        wait_scatter(i - 3, yslot)

    @pl.when((i >= 1) & (i < na) & (nv_prev < TM_MOE))
    def _():
        lax.fori_loop(0, nv_prev, lambda r, c: (start(scatter_copy(i - 1, yprev, r)), c)[1], 0)

    def block_body(scatter_prev):
        wait_gather(slot)
        xu = _load_row_tiles(xbuf.at[slot], TM_MOE)
        lo = pltpu.bitcast(xu << 16, F32).astype(BF16)
        hi = pltpu.bitcast(xu & jnp.uint32(0xFFFF0000), F32).astype(BF16)
        nxt = jnp.minimum(i + 1, na - 1)
        glink = lax.shift_right_logical(na, 31)
        slink = glink
        for r in range(TM_MOE):
            src = gs_ref[nxt * TM_MOE + r + glink]
            start(gather_copy(nxt, other, r, src), priority=0)
            glink = lax.shift_right_logical(src, 31)
            if scatter_prev:
                dst = sd_ref[(i - 1) * TM_MOE + r + slink]
                start(scatter_copy(i - 1, yprev, r, dst), priority=1)
                slink = lax.shift_right_logical(dst, 31)
        fc = 256
        for c in range(D_EXPERT // fc):
            cs = slice(c * fc, (c + 1) * fc)
            g = (jnp.dot(lo, wg_ref[0:D_HALF, cs], preferred_element_type=F32)
                 + jnp.dot(hi, wg_ref[D_HALF:D_MODEL, cs], preferred_element_type=F32))
            u = (jnp.dot(lo, wu_ref[0:D_HALF, cs], preferred_element_type=F32)
                 + jnp.dot(hi, wu_ref[D_HALF:D_MODEL, cs], preferred_element_type=F32))
            a = (g * jax.nn.sigmoid(g) * u).astype(BF16)
            part = jnp.dot(a, wd_ref[cs, :], preferred_element_type=F32)
            if c == 0:
                yacc[...] = part
            else:
                yacc[...] += part
        _store_row_tiles(ybuf.at[yslot], _pack_pairs(yacc[...]))

    @pl.when((i < na) & prev_full)
    def _():
        block_body(True)

    @pl.when((i < na) & jnp.logical_not(prev_full))
    def _():
        block_body(False)

    @pl.when(i == na - 1)
    def _():
        lax.fori_loop(0, nv_ref[i], lambda r, c: (start(scatter_copy(i, yslot, r)), c)[1], 0)
        wait_gather(other)
        wait_scatter(i, yslot)

        @pl.when(i >= 1)
        def _():
            wait_scatter(i - 1, yprev)

        @pl.when(i >= 2)
        def _():
            wait_scatter(i - 2, lax.rem(i + 1, 3))


def _moe(l, blk_exp, n_active, n_valid, gsrc, sdst, hp, wg, wu, wd):
    def w_blk(i, be, na, nv, gs, sd):
        return (l, be[i], 0, 0)

    return pl.pallas_call(
        _moe_body,
        grid_spec=pltpu.PrefetchScalarGridSpec(
            num_scalar_prefetch=5,
            grid=(N_BLOCKS,),
            in_specs=[
                pl.BlockSpec(memory_space=pl.ANY),
                pl.BlockSpec((None, None, D_MODEL, D_EXPERT), w_blk),
                pl.BlockSpec((None, None, D_MODEL, D_EXPERT), w_blk),
                pl.BlockSpec((None, None, D_EXPERT, D_MODEL), w_blk),
            ],
            out_specs=pl.BlockSpec(memory_space=pl.ANY),
            scratch_shapes=[
                pltpu.VMEM((D_MODEL, D_EXPERT), BF16),
                pltpu.VMEM((D_MODEL, D_EXPERT), BF16),
                pltpu.VMEM((D_EXPERT, D_MODEL), BF16),
                pltpu.VMEM((2, TM_MOE * ROW_TILE, LANES), jnp.uint32),
                pltpu.VMEM((3, TM_MOE * ROW_TILE, LANES), jnp.uint32),
                pltpu.VMEM((TM_MOE, D_MODEL), F32),
                pltpu.SemaphoreType.DMA((2,)),
                pltpu.SemaphoreType.DMA((3,)),
            ],
        ),
        out_shape=jax.ShapeDtypeStruct((2 * N_TOK * ROW_TILE, LANES), jnp.uint32),
        compiler_params=_params(("arbitrary",), vmem=60 * 1024 * 1024),
        name="moe",
    )(blk_exp, n_active, n_valid, gsrc, sdst, hp, wg, wu, wd)


def _final_body(x_ref, y0_ref, y1_ref, rg_ref, g_ref, o_ref):
    y = _moe_residual(x_ref, y0_ref, y1_ref, rg_ref)
    ms = jnp.mean(y * y, axis=-1, keepdims=True)
    o_ref[...] = (y * lax.rsqrt(ms + EPS)) * g_ref[...]


def _final(x, yk, rg, g):
    tm = 512
    row = pl.BlockSpec((tm, D_MODEL), lambda i: (i, 0))
    return pl.pallas_call(
        _final_body,
        grid=(N_TOK // tm,),
        in_specs=[
            row,
            pl.BlockSpec((None, tm * ROW_TILE, LANES), lambda i: (0, i, 0)),
            pl.BlockSpec((None, tm * ROW_TILE, LANES), lambda i: (1, i, 0)),
            pl.BlockSpec((tm, LANES), lambda i: (i, 0)),
            pl.BlockSpec((1, D_MODEL), lambda i: (0, 0)),
        ],
        out_specs=row,
        out_shape=jax.ShapeDtypeStruct((N_TOK, D_MODEL), F32),
        compiler_params=_params(("parallel",)),
        name="final",
    )(x, yk, yk, rg, g)


def _plan(ri, cnt):
    counts = cnt[0, ROUTE_OFF:ROUTE_OFF + N_EXPERTS].astype(jnp.int32)
    nblk = (counts + TM_MOE - 1) // TM_MOE
    blk_end = jnp.cumsum(nblk)
    blk_start = blk_end - nblk
    pad_start = blk_start * TM_MOE
    n_active = blk_end[-1:]
    blk = jnp.minimum(jnp.arange(N_BLOCKS, dtype=jnp.int32), n_active[0] - 1)
    blk_exp = jnp.sum(blk_end[None, :] <= blk[:, None], axis=1)
    n_valid = jnp.clip(counts[blk_exp] - (blk - blk_start[blk_exp]) * TM_MOE, 0, TM_MOE)
    eid, rank = ri[:, 0:2], ri[:, 2:4]
    onehot = eid[:, :, None] == jnp.arange(N_EXPERTS, dtype=jnp.int32)[None, None, :]
    dest = rank + jnp.sum(jnp.where(onehot, pad_start[None, None, :], 0), axis=-1)
    pad_lo = jnp.concatenate([pad_start + counts, n_active * TM_MOE])
    pad_hi = jnp.concatenate([blk_end * TM_MOE, jnp.full((1,), N_ROWS, jnp.int32)])
    pad_ranges = jnp.stack([pad_lo, pad_hi], axis=1).reshape(-1).astype(jnp.int32)
    rowv = _invert(dest.reshape(-1).astype(jnp.int32), pad_ranges)
    gsrc = (rowv >> 1) * ROW_TILE
    sdst = gsrc + (rowv & 1) * (N_TOK * ROW_TILE)
    return (blk_exp.astype(jnp.int32), n_active.astype(jnp.int32), n_valid.astype(jnp.int32),
            gsrc, sdst)


def kernel(x, mix_norm, w_in, w_fourier, w_spatial, b_spatial, conv_w, w_pool, pool_scale,
           out_norm, w_out, ffn_norm, w_group, b_group, w_router, b_router, w_gate, w_up,
           w_down, final_norm):
    cs, m2 = _fourier_consts()
    tr = 512
    tri = (jnp.arange(tr)[:, None] > jnp.arange(tr)[None, :]).astype(BF16)

    w_in_b = w_in.astype(BF16)
    w_f_b = w_fourier.astype(BF16)
    w_s_b = w_spatial.astype(BF16)
    w_p_b = w_pool.astype(BF16)
    w_o_b = w_out.astype(BF16)
    pad = LANES - N_EXPERT_GROUPS - N_EXPERTS
    wr = jnp.concatenate([w_group, w_router, jnp.zeros((DEPTH, D_MODEL, pad), F32)], axis=-1)
    wr_hi = wr.astype(BF16)
    wr_lo = (wr - wr_hi.astype(F32)).astype(BF16)
    br = jnp.concatenate([b_group, b_router, jnp.zeros((DEPTH, pad), F32)], axis=-1)[:, None, :]
    bsb = jnp.broadcast_to(b_spatial[:, :, :, None], (DEPTH, N_HEADS, CHUNK, HEAD_DIM))
    mix_g = mix_norm[:, None, :]
    ffn_g = ffn_norm[:, None, :]
    out_g = out_norm.reshape(DEPTH, 1, D_MODEL)
    pool_s = pool_scale[:, None, :]

    xt = x.reshape(N_TOK, D_MODEL)
    moe = None
    for l in range(DEPTH):
        if moe is None:
            z = _inproj(l, xt, mix_g, w_in_b)
        else:
            z, xt = _inproj(l, xt, mix_g, w_in_b, moe=moe)
        z3 = z.reshape(BATCH, SEQ, D_IN_PROJ)
        f3 = _fourier(z3, cs, m2)
        x3 = _mixer(l, xt.reshape(BATCH, SEQ, D_MODEL), f3, z3, w_f_b, w_s_b, bsb,
                    conv_w, w_p_b, pool_s, out_g, w_o_b)
        xt = x3.reshape(N_TOK, D_MODEL)
        hp, ri, rg, cnt = _router(l, xt, ffn_g, wr_hi, wr_lo, br, tri)
        blk_exp, n_active, n_valid, gsrc, sdst = _plan(ri, cnt)
        yk = _moe(l, blk_exp, n_active, n_valid, gsrc, sdst, hp, w_gate, w_up, w_down)
        moe = (yk.reshape(2, N_TOK * ROW_TILE, LANES), rg)
    xt = _final(xt, moe[0], moe[1], final_norm[None, :])
    return xt.reshape(BATCH, SEQ, D_MODEL)
```

```python
import functools

import numpy as np
import jax
import jax.numpy as jnp
from jax import lax
from jax.experimental import pallas as pl
from jax.experimental.pallas import tpu as pltpu

D_MODEL = 2048
BATCH = 4
SEQ = 4096
DEPTH = 4
N_TOK = BATCH * SEQ
D_GROUP = 512
HEAD_DIM = 128
N_HEADS = 4
CHUNK = 128
POOL_WINDOWS = (2, 4, 8, 16)
D_IN_PROJ = 7 * D_GROUP
N_EXPERT_GROUPS = 4
EXPERTS_PER_GROUP = 8
N_EXPERTS = 32
D_EXPERT = 768
EPS = 1e-6

F32 = jnp.float32
BF16 = jnp.bfloat16

FFT_Q = 8
FFT_P = SEQ // FFT_Q
FFT_HEADS = 2
FFT_W = FFT_HEADS * HEAD_DIM
HALO = 16
LANES = 128
ROUTE_OFF = N_EXPERT_GROUPS

TM_MOE = 256
N_BLOCKS = (2 * N_TOK) // TM_MOE + N_EXPERTS
N_ROWS = N_BLOCKS * TM_MOE
D_HALF = D_MODEL // 2
ROW_TILE = D_HALF // LANES
ROW_DMA_PRIORITY = 1

VMEM_LIMIT = 56 * 1024 * 1024


def _params(sem, vmem=VMEM_LIMIT):
    return pltpu.CompilerParams(dimension_semantics=sem, vmem_limit_bytes=vmem)


def _unpack_pairs(u):
    lo = pltpu.bitcast(u << 16, F32)
    hi = pltpu.bitcast(u & jnp.uint32(0xFFFF0000), F32)
    return jnp.concatenate([lo, hi], axis=1)


def _pack_pairs(v):
    bits = pltpu.bitcast(v.astype(BF16).astype(F32), jnp.uint32)
    return (bits[:, :D_HALF] >> 16) | (bits[:, D_HALF:] & jnp.uint32(0xFFFF0000))


def _store_row_tiles(ref, packed):
    rows = packed.shape[0]
    for s in range(ROW_TILE):
        ref[pl.ds(s, rows, stride=ROW_TILE), :] = packed[:, s * LANES:(s + 1) * LANES]


def _load_row_tiles(ref, rows):
    return jnp.concatenate(
        [ref[pl.ds(s, rows, stride=ROW_TILE), :] for s in range(ROW_TILE)], axis=1)


def _moe_residual(x_ref, y0_ref, y1_ref, rg_ref):
    rg = rg_ref[...]
    rows = x_ref.shape[0]
    return (x_ref[...] + rg[:, 0:1] * _unpack_pairs(_load_row_tiles(y0_ref, rows))
            + rg[:, 1:2] * _unpack_pairs(_load_row_tiles(y1_ref, rows)))


def _project(x, g_ref, w_ref, z_ref):
    ms = jnp.mean(x * x, axis=-1, keepdims=True)
    h = ((x * lax.rsqrt(ms + EPS)) * g_ref[...]).astype(BF16)
    tn = 512
    for j in range(D_IN_PROJ // tn):
        cs = slice(j * tn, (j + 1) * tn)
        z_ref[:, cs] = jnp.dot(h, w_ref[:, cs], preferred_element_type=F32).astype(BF16)


def _inproj_body(x_ref, g_ref, w_ref, z_ref):
    _project(x_ref[...], g_ref, w_ref, z_ref)


def _inproj_moe_body(x_ref, y0_ref, y1_ref, rg_ref, g_ref, w_ref, z_ref, xo_ref):
    x = _moe_residual(x_ref, y0_ref, y1_ref, rg_ref)
    xo_ref[...] = x
    _project(x, g_ref, w_ref, z_ref)


def _inproj(l, x, g, w, moe=None):
    tm = 512
    row = pl.BlockSpec((tm, D_MODEL), lambda i: (i, 0))
    w_specs = [
        pl.BlockSpec((None, 1, D_MODEL), lambda i: (l, 0, 0)),
        pl.BlockSpec((None, D_MODEL, D_IN_PROJ), lambda i: (l, 0, 0), pipeline_mode=pl.Buffered(1)),
    ]
    z_spec = pl.BlockSpec((tm, D_IN_PROJ), lambda i: (i, 0))
    z_shape = jax.ShapeDtypeStruct((N_TOK, D_IN_PROJ), BF16)
    if moe is None:
        return pl.pallas_call(
            _inproj_body, grid=(N_TOK // tm,), in_specs=[row] + w_specs, out_specs=z_spec,
            out_shape=z_shape, compiler_params=_params(("parallel",)), name="inproj",
        )(x, g, w)
    yk, rg = moe
    return pl.pallas_call(
        _inproj_moe_body,
        grid=(N_TOK // tm,),
        in_specs=[
            row,
            pl.BlockSpec((None, tm * ROW_TILE, LANES), lambda i: (0, i, 0)),
            pl.BlockSpec((None, tm * ROW_TILE, LANES), lambda i: (1, i, 0)),
            pl.BlockSpec((tm, LANES), lambda i: (i, 0)),
        ] + w_specs,
        out_specs=[z_spec, row],
        out_shape=[z_shape, jax.ShapeDtypeStruct((N_TOK, D_MODEL), F32)],
        compiler_params=_params(("parallel",)),
        name="inproj_moe",
    )(x, yk, yk, rg, g, w)


def _fourier_consts():
    c = np.arange(HEAD_DIM)
    ang = 2.0 * np.pi * (np.outer(c, c) % HEAD_DIM) / HEAD_DIM
    cc, sc = np.cos(ang), np.sin(ang)
    scale = 1.0 / np.sqrt(SEQ * HEAD_DIM)
    cs = np.block([[cc, -sc], [sc, cc]]) * scale
    s1 = np.arange(FFT_P)
    m2 = np.zeros((FFT_Q, FFT_P, 2 * FFT_P), np.float64)
    for k2 in range(FFT_Q):
        k = FFT_Q * np.arange(FFT_P) + k2
        th = 2.0 * np.pi * (np.outer(k, s1) % SEQ) / SEQ
        m2[k2, :, :FFT_P] = np.cos(th)
        m2[k2, :, FFT_P:] = np.sin(th)
    return jnp.asarray(cs, BF16), jnp.asarray(m2, BF16)


def _fourier_body(z_ref, cs_ref, m2_ref, f_ref, y_scr, u_scr):
    rc = 64
    r = np.float32(np.sqrt(0.5))

    def chunk(ci, carry):
        r0 = pl.multiple_of(ci * rc, rc)
        zb = [z_ref[pl.ds(s2 * FFT_P + r0, rc), :].astype(F32) for s2 in range(FFT_Q)]
        e0, e1 = zb[0] + zb[4], zb[0] - zb[4]
        e2, e3 = zb[2] + zb[6], zb[2] - zb[6]
        o0, o1 = zb[1] + zb[5], zb[1] - zb[5]
        o2, o3 = zb[3] + zb[7], zb[3] - zb[7]
        p, q = (o1 - o3) * r, (o1 + o3) * r
        ee, oo = e0 + e2, o0 + o2
        ed, od = e0 - e2, o0 - o2
        zero = jnp.zeros_like(e0)
        ys = [(ee + oo, zero), (e1 + p, -e3 - q), (ed, -od), (e1 - p, e3 - q),
              (ee - oo, zero), (e1 - p, q - e3), (ed, od), (e1 + p, e3 + q)]
        for k2 in range(FFT_Q):
            re, im = ys[k2]
            rows = pl.ds(k2 * FFT_P + r0, rc)
            for h in range(FFT_HEADS):
                hs = slice(h * HEAD_DIM, (h + 1) * HEAD_DIM)
                y_scr[rows, 2 * h * HEAD_DIM:(2 * h + 1) * HEAD_DIM] = re[:, hs].astype(BF16)
                y_scr[rows, (2 * h + 1) * HEAD_DIM:(2 * h + 2) * HEAD_DIM] = im[:, hs].astype(BF16)
        return carry

    lax.fori_loop(0, FFT_P // rc, chunk, 0)

    for k2 in range(FFT_Q):
        for h in range(FFT_HEADS):
            yk = y_scr[k2 * FFT_P:(k2 + 1) * FFT_P, 2 * h * HEAD_DIM:(2 * h + 2) * HEAD_DIM]
            ab = jnp.dot(yk, cs_ref[...], preferred_element_type=F32)
            hs = slice(h * HEAD_DIM, (h + 1) * HEAD_DIM)
            u_scr[k2, 0:FFT_P, hs] = ab[:, :HEAD_DIM].astype(BF16)
            u_scr[k2, FFT_P:2 * FFT_P, hs] = ab[:, HEAD_DIM:].astype(BF16)

    for k2 in range(FFT_Q):
        res = jnp.dot(m2_ref[k2], u_scr[k2], preferred_element_type=F32)
        for h in range(FFT_HEADS):
            f_ref[h, pl.ds(k2, FFT_P, stride=FFT_Q), :] = res[:, h * HEAD_DIM:(h + 1) * HEAD_DIM]


def _fourier(z3, cs, m2):
    return pl.pallas_call(
        _fourier_body,
        grid=(BATCH, N_HEADS // FFT_HEADS),
        in_specs=[
            pl.BlockSpec((None, SEQ, FFT_W), lambda b, h: (b, 0, h)),
            pl.BlockSpec((2 * HEAD_DIM, 2 * HEAD_DIM), lambda b, h: (0, 0)),
            pl.BlockSpec((FFT_Q, FFT_P, 2 * FFT_P), lambda b, h: (0, 0, 0)),
        ],
        out_specs=pl.BlockSpec((None, FFT_HEADS, SEQ, HEAD_DIM), lambda b, h: (b, h, 0, 0)),
        out_shape=jax.ShapeDtypeStruct((BATCH, N_HEADS, SEQ, HEAD_DIM), F32),
        scratch_shapes=[
            pltpu.VMEM((SEQ, 2 * FFT_W), BF16),
            pltpu.VMEM((FFT_Q, 2 * FFT_P, FFT_W), BF16),
        ],
        compiler_params=_params(("parallel", "parallel")),
        name="fourier",
    )(z3, cs, m2)


def _gelu(x):
    return 0.5 * x * (1.0 + lax.erf(x * np.float32(np.sqrt(0.5))))


def _mixer_body(x_ref, f_ref, zu_ref, zv_ref, zb_ref, zc_ref, zval_ref, zp_ref,
                zc_prev, zval_prev, zp_prev, zc_next, zval_next, zp_next,
                wf_ref, ws_ref, bs_ref, cw_ref, wp_ref, ps_ref, on_ref, wo_ref,
                o_ref, ext_a, ext_p, ybf):
    i = pl.program_id(1)
    ts = x_ref.shape[0]
    keep_prev = (i > 0).astype(F32)
    keep_next = (i < pl.num_programs(1) - 1).astype(F32)

    def norm_store(y, g):
        gs = slice(g * D_GROUP, (g + 1) * D_GROUP)
        ms = jnp.mean(y * y, axis=-1, keepdims=True)
        ybf[:, gs] = ((y * lax.rsqrt(ms + EPS)) * on_ref[:, gs]).astype(BF16)

    spec = jnp.concatenate([f_ref[h] for h in range(N_HEADS)], axis=1).astype(BF16)
    norm_store(jnp.dot(spec, wf_ref[...], preferred_element_type=F32), 0)

    gu = _gelu(zu_ref[...].astype(F32))
    gv = _gelu(zv_ref[...].astype(F32)).astype(BF16)
    cols = []
    for h in range(N_HEADS):
        hs = slice(h * HEAD_DIM, (h + 1) * HEAD_DIM)
        rows = []
        for n in range(ts // CHUNK):
            vv = gv[n * CHUNK:(n + 1) * CHUNK, hs]
            rows.append(jnp.dot(ws_ref[h], vv, preferred_element_type=F32) + bs_ref[h])
        cols.append(jnp.concatenate(rows, axis=0))
    norm_store(gu * jnp.concatenate(cols, axis=1), 1)

    def f32(ref):
        return ref[...].astype(F32)

    a = f32(zc_ref) * f32(zval_ref)
    ext_a[0:HALO, :] = f32(zc_prev) * f32(zval_prev) * keep_prev
    ext_a[HALO:HALO + ts, :] = a
    ext_a[HALO + ts:2 * HALO + ts, :] = f32(zc_next) * f32(zval_next) * keep_next
    conv = (cw_ref[0:1, :] * ext_a[HALO - 1:HALO - 1 + ts, :] + cw_ref[1:2, :] * a
            + cw_ref[2:3, :] * ext_a[HALO + 1:HALO + 1 + ts, :])
    norm_store(f32(zb_ref) * conv, 2)

    zp = f32(zp_ref)
    ext_p[0:HALO, :] = f32(zp_prev) * keep_prev
    ext_p[HALO:HALO + ts, :] = zp
    ext_p[HALO + ts:2 * HALO + ts, :] = f32(zp_next) * keep_next
    t = i * ts + lax.broadcasted_iota(jnp.int32, (ts, HEAD_DIM), 0)
    outs = []
    for g, w in enumerate(POOL_WINDOWS):
        gs = slice(g * HEAD_DIM, (g + 1) * HEAD_DIM)
        acc = ext_p[HALO - w // 2:HALO - w // 2 + ts, gs]
        for d in range(-w // 2 + 1, w // 2):
            acc = acc + ext_p[HALO + d:HALO + d + ts, gs]
        cnt = (jnp.minimum(t + w // 2, SEQ) - jnp.maximum(t - w // 2, 0)).astype(F32)
        pg = acc / cnt - zp[:, gs]
        outs.append(jnp.dot(pg.astype(BF16), wp_ref[g], preferred_element_type=F32))
    norm_store(jnp.concatenate(outs, axis=1) * ps_ref[...], 3)

    o_ref[...] = x_ref[...] + jnp.dot(ybf[...], wo_ref[...], preferred_element_type=F32)


def _mixer(l, x3, f3, z3, wf, ws, bsb, cw, wp, ps, on, wo):
    ts = 512
    nb8 = ts // HALO
    last8 = SEQ // HALO - 1

    def col(j):
        return pl.BlockSpec((None, ts, D_GROUP), lambda b, i, j=j: (b, i, j))

    def prev(j):
        return pl.BlockSpec((None, HALO, D_GROUP),
                            lambda b, i, j=j: (b, jnp.maximum(i * nb8 - 1, 0), j))

    def nxt(j):
        return pl.BlockSpec((None, HALO, D_GROUP),
                            lambda b, i, j=j: (b, jnp.minimum((i + 1) * nb8, last8), j))

    def full(shape, **kw):
        return pl.BlockSpec((None,) + shape, lambda b, i: (l,) + (0,) * len(shape), **kw)

    return pl.pallas_call(
        _mixer_body,
        grid=(BATCH, SEQ // ts),
        in_specs=[
            pl.BlockSpec((None, ts, D_MODEL), lambda b, i: (b, i, 0)),
            pl.BlockSpec((None, N_HEADS, ts, HEAD_DIM), lambda b, i: (b, 0, i, 0)),
            col(1), col(2), col(3), col(4), col(5), col(6),
            prev(4), prev(5), prev(6), nxt(4), nxt(5), nxt(6),
            full((D_GROUP, D_GROUP)), full((N_HEADS, CHUNK, CHUNK)),
            full((N_HEADS, CHUNK, HEAD_DIM)), full((3, D_GROUP)),
            full((4, HEAD_DIM, HEAD_DIM)), full((1, D_GROUP)), full((1, D_MODEL)),
            full((D_MODEL, D_MODEL), pipeline_mode=pl.Buffered(1)),
        ],
        out_specs=pl.BlockSpec((None, ts, D_MODEL), lambda b, i: (b, i, 0)),
        out_shape=jax.ShapeDtypeStruct((BATCH, SEQ, D_MODEL), F32),
        scratch_shapes=[
            pltpu.VMEM((ts + 2 * HALO, D_GROUP), F32),
            pltpu.VMEM((ts + 2 * HALO, D_GROUP), F32),
            pltpu.VMEM((ts, D_MODEL), BF16),
        ],
        compiler_params=_params(("parallel", "arbitrary")),
        name="mixer",
    )(x3, f3, z3, z3, z3, z3, z3, z3, z3, z3, z3, z3, z3, z3, wf, ws, bsb, cw, wp, ps, on, wo)


def _router_body(x_ref, g_ref, wr_ref, wrl_ref, br_ref, tri_ref, hp_ref, ri_ref, rg_ref, cnt_ref,
                 carry):
    @pl.when(pl.program_id(0) == 0)
    def _():
        carry[...] = jnp.zeros_like(carry)

    x = x_ref[...]
    tr = x.shape[0]
    ms = jnp.mean(x * x, axis=-1, keepdims=True)
    h = (x * lax.rsqrt(ms + EPS)) * g_ref[...]

    _store_row_tiles(hp_ref, _pack_pairs(h))

    h_hi = h.astype(BF16)
    h_lo = (h - h_hi.astype(F32)).astype(BF16)
    logits = (jnp.dot(h_hi, wr_ref[...], preferred_element_type=F32)
              + jnp.dot(h_lo, wr_ref[...], preferred_element_type=F32)
              + jnp.dot(h_hi, wrl_ref[...], preferred_element_type=F32)) + br_ref[...]
    lane = lax.broadcasted_iota(jnp.int32, (tr, LANES), 1)
    neg = np.float32(-np.inf)

    def first_argmax(v, vmax):
        return jnp.min(jnp.where(v == vmax, lane, LANES), axis=-1, keepdims=True)

    mg = jnp.where(lane < N_EXPERT_GROUPS, logits, neg)
    gmax = jnp.max(mg, axis=-1, keepdims=True)
    grp = first_argmax(mg, gmax)
    p_grp = 1.0 / jnp.sum(jnp.exp(mg - gmax), axis=-1, keepdims=True)

    lo = ROUTE_OFF + EXPERTS_PER_GROUP * grp
    le = jnp.where(lane >= lo, jnp.where(lane < lo + EXPERTS_PER_GROUP, logits, neg), neg)
    emax = jnp.max(le, axis=-1, keepdims=True)
    i1 = first_argmax(le, emax)
    le2 = jnp.where(lane == i1, neg, le)
    emax2 = jnp.max(le2, axis=-1, keepdims=True)
    i2 = first_argmax(le2, emax2)
    e2 = jnp.exp(emax2 - emax)
    gate1 = p_grp / (1.0 + e2)
    gate2 = p_grp * e2 / (1.0 + e2)

    is1, is2 = lane == i1, lane == i2
    oh = jnp.where(is1, 1.0, jnp.where(is2, 1.0, 0.0))
    pref = jnp.dot(tri_ref[...], oh.astype(BF16), preferred_element_type=F32) + carry[...]
    r1 = jnp.sum(jnp.where(is1, pref, 0.0), axis=-1, keepdims=True).astype(jnp.int32)
    r2 = jnp.sum(jnp.where(is2, pref, 0.0), axis=-1, keepdims=True).astype(jnp.int32)
    carry[...] = carry[...] + jnp.sum(oh, axis=0, keepdims=True)
    cnt_ref[...] = carry[...]

    ri_ref[...] = jnp.where(lane == 0, i1 - ROUTE_OFF,
                            jnp.where(lane == 1, i2 - ROUTE_OFF,
                                      jnp.where(lane == 2, r1, jnp.where(lane == 3, r2, 0))))
    rg_ref[...] = jnp.where(lane == 0, gate1, jnp.where(lane == 1, gate2, 0.0))


def _router(l, x, g, wr, wrl, br, tri):
    tr = tri.shape[0]
    return pl.pallas_call(
        _router_body,
        grid=(N_TOK // tr,),
        in_specs=[
            pl.BlockSpec((tr, D_MODEL), lambda i: (i, 0)),
            pl.BlockSpec((None, 1, D_MODEL), lambda i: (l, 0, 0)),
            pl.BlockSpec((None, D_MODEL, LANES), lambda i: (l, 0, 0)),
            pl.BlockSpec((None, D_MODEL, LANES), lambda i: (l, 0, 0)),
            pl.BlockSpec((None, 1, LANES), lambda i: (l, 0, 0)),
            pl.BlockSpec((tr, tr), lambda i: (0, 0)),
        ],
        out_specs=[
            pl.BlockSpec((tr * ROW_TILE, LANES), lambda i: (i, 0)),
            pl.BlockSpec((tr, LANES), lambda i: (i, 0)),
            pl.BlockSpec((tr, LANES), lambda i: (i, 0)),
            pl.BlockSpec((1, LANES), lambda i: (0, 0)),
        ],
        out_shape=[
            jax.ShapeDtypeStruct((N_TOK * ROW_TILE, LANES), jnp.uint32),
            jax.ShapeDtypeStruct((N_TOK, LANES), jnp.int32),
            jax.ShapeDtypeStruct((N_TOK, LANES), F32),
            jax.ShapeDtypeStruct((1, LANES), F32),
        ],
        scratch_shapes=[pltpu.VMEM((1, LANES), F32)],
        compiler_params=_params(("arbitrary",)),
        name="router",
    )(x, g, wr, wrl, br, tri)


def _invert_body(dest_ref, pad_ref, rowv_ref):
    def fill(j, c):
        def zero(r, c2):
            rowv_ref[r] = 0
            return c2

        lax.fori_loop(pad_ref[2 * j], pad_ref[2 * j + 1], zero, 0)
        return c

    lax.fori_loop(0, N_EXPERTS + 1, fill, 0)

    unroll = 16

    def body(j, c):
        rows = [dest_ref[j * unroll + u] for u in range(unroll)]
        for u in range(unroll):
            rowv_ref[rows[u]] = j * unroll + u
        return c

    lax.fori_loop(0, 2 * N_TOK // unroll, body, 0)


def _invert(dest, pad_ranges):
    return pl.pallas_call(
        _invert_body,
        grid_spec=pltpu.PrefetchScalarGridSpec(
            num_scalar_prefetch=2,
            grid=(1,),
            in_specs=[],
            out_specs=pl.BlockSpec(memory_space=pltpu.SMEM),
        ),
        out_shape=jax.ShapeDtypeStruct((N_ROWS,), jnp.int32),
        compiler_params=_params(("arbitrary",)),
        name="invert",
    )(dest, pad_ranges)


def _moe_body(be_ref, na_ref, nv_ref, par_ref, nxe_ref, gs_ref, sd_ref, hp_ref, wg_hbm, wu_hbm,
              wd_hbm, yk_ref, wg32, wu32, wd32, wg_ref, wu_ref, wd_ref, xbuf, ybuf, yacc, gsem, ssem,
              wsem, *, layer):
    i = pl.program_id(0)
    na = na_ref[0]

    def weight_copies(e, p):
        return [pltpu.make_async_copy(src.at[layer, e], dst.at[p], wsem.at[p])
                for src, dst in ((wg_hbm, wg32), (wu_hbm, wu32), (wd_hbm, wd32))]
    slot = lax.rem(i, 2)
    other = 1 - slot
    blk_rows = TM_MOE * ROW_TILE

    def tile_rows(r):
        start = r * ROW_TILE
        return pl.ds(start if isinstance(r, int) else pl.multiple_of(start, ROW_TILE), ROW_TILE)

    def gather_copy(blk, s, r, src=None):
        if src is None:
            src = gs_ref[blk * TM_MOE + r]
        return pltpu.make_async_copy(hp_ref.at[pl.ds(pl.multiple_of(src, ROW_TILE), ROW_TILE), :],
                                     xbuf.at[s, tile_rows(r), :], gsem.at[s])

    def scatter_copy(blk, s, r, dst=None):
        if dst is None:
            dst = sd_ref[blk * TM_MOE + r]
        return pltpu.make_async_copy(ybuf.at[s, tile_rows(r), :],
                                     yk_ref.at[pl.ds(pl.multiple_of(dst, ROW_TILE), ROW_TILE), :],
                                     ssem.at[s])

    def wait_gather(s):
        pltpu.make_async_copy(hp_ref.at[pl.ds(0, blk_rows), :], xbuf.at[s], gsem.at[s]).wait()

    def wait_scatter(blk, s):
        n = nv_ref[blk]

        @pl.when(n == TM_MOE)
        def _():
            pltpu.make_async_copy(ybuf.at[s], yk_ref.at[pl.ds(0, blk_rows), :], ssem.at[s]).wait()

        @pl.when(n < TM_MOE)
        def _():
            lax.fori_loop(0, n, lambda r, c: (scatter_copy(blk, s, r).wait(), c)[1], 0)

    yslot = lax.rem(i, 3)
    yprev = lax.rem(i + 2, 3)
    nv_prev = nv_ref[jnp.maximum(i - 1, 0)]
    prev_full = (i >= 1) & (nv_prev == TM_MOE)

    def start(copy, priority=ROW_DMA_PRIORITY):
        copy.start(priority=priority)

    @pl.when(i == 0)
    def _():
        for c in weight_copies(be_ref[0], 0):
            c.start()
        lax.fori_loop(0, TM_MOE, lambda r, c: (start(gather_copy(0, 0, r)), c)[1], 0)

    @pl.when((i < na) & ((i == 0) | (be_ref[i] != be_ref[jnp.maximum(i - 1, 0)])))
    def _():
        p = par_ref[i]
        for c in weight_copies(be_ref[i], p):
            c.wait()

        @pl.when(nxe_ref[i] >= 0)
        def _():
            for c in weight_copies(nxe_ref[i], 1 - p):
                c.start()

        wg_ref[...] = wg32[p].astype(BF16)
        wu_ref[...] = wu32[p].astype(BF16)
        wd_ref[...] = wd32[p].astype(BF16)

    @pl.when((i >= 3) & (i < na))
    def _():
        wait_scatter(i - 3, yslot)

    @pl.when((i >= 1) & (i < na) & (nv_prev < TM_MOE))
    def _():
        lax.fori_loop(0, nv_prev, lambda r, c: (start(scatter_copy(i - 1, yprev, r)), c)[1], 0)

    def block_body(scatter_prev):
        wait_gather(slot)
        xu = _load_row_tiles(xbuf.at[slot], TM_MOE)
        lo = pltpu.bitcast(xu << 16, F32).astype(BF16)
        hi = pltpu.bitcast(xu & jnp.uint32(0xFFFF0000), F32).astype(BF16)
        nxt = jnp.minimum(i + 1, na - 1)
        for r in range(TM_MOE):
            start(gather_copy(nxt, other, r), priority=r % 2)
            if scatter_prev:
                start(scatter_copy(i - 1, yprev, r), priority=(r + 1) % 2)
        fc = 256
        for c in range(D_EXPERT // fc):
            cs = slice(c * fc, (c + 1) * fc)
            g = (jnp.dot(lo, wg_ref[0:D_HALF, cs], preferred_element_type=F32)
                 + jnp.dot(hi, wg_ref[D_HALF:D_MODEL, cs], preferred_element_type=F32))
            u = (jnp.dot(lo, wu_ref[0:D_HALF, cs], preferred_element_type=F32)
                 + jnp.dot(hi, wu_ref[D_HALF:D_MODEL, cs], preferred_element_type=F32))
            a = (g * jax.nn.sigmoid(g) * u).astype(BF16)
            part = jnp.dot(a, wd_ref[cs, :], preferred_element_type=F32)
            if c == 0:
                yacc[...] = part
            else:
                yacc[...] += part
        _store_row_tiles(ybuf.at[yslot], _pack_pairs(yacc[...]))

    @pl.when((i < na) & prev_full)
    def _():
        block_body(True)

    @pl.when((i < na) & jnp.logical_not(prev_full))
    def _():
        block_body(False)

    @pl.when(i == na - 1)
    def _():
        lax.fori_loop(0, nv_ref[i], lambda r, c: (start(scatter_copy(i, yslot, r)), c)[1], 0)
        wait_gather(other)
        wait_scatter(i, yslot)

        @pl.when(i >= 1)
        def _():
            wait_scatter(i - 1, yprev)

        @pl.when(i >= 2)
        def _():
            wait_scatter(i - 2, lax.rem(i + 1, 3))


def _moe(l, blk_exp, n_active, n_valid, w_slot, next_exp, gsrc, sdst, hp, wg, wu, wd):
    return pl.pallas_call(
        functools.partial(_moe_body, layer=l),
        grid_spec=pltpu.PrefetchScalarGridSpec(
            num_scalar_prefetch=7,
            grid=(N_BLOCKS,),
            in_specs=[pl.BlockSpec(memory_space=pl.ANY)] * 4,
            out_specs=pl.BlockSpec(memory_space=pl.ANY),
            scratch_shapes=[
                pltpu.VMEM((2, D_MODEL, D_EXPERT), F32),
                pltpu.VMEM((2, D_MODEL, D_EXPERT), F32),
                pltpu.VMEM((2, D_EXPERT, D_MODEL), F32),
                pltpu.VMEM((D_MODEL, D_EXPERT), BF16),
                pltpu.VMEM((D_MODEL, D_EXPERT), BF16),
                pltpu.VMEM((D_EXPERT, D_MODEL), BF16),
                pltpu.VMEM((2, TM_MOE * ROW_TILE, LANES), jnp.uint32),
                pltpu.VMEM((3, TM_MOE * ROW_TILE, LANES), jnp.uint32),
                pltpu.VMEM((TM_MOE, D_MODEL), F32),
                pltpu.SemaphoreType.DMA((2,)),
                pltpu.SemaphoreType.DMA((3,)),
                pltpu.SemaphoreType.DMA((2,)),
            ],
        ),
        out_shape=jax.ShapeDtypeStruct((2 * N_TOK * ROW_TILE, LANES), jnp.uint32),
        compiler_params=_params(("arbitrary",), vmem=60 * 1024 * 1024),
        name="moe",
    )(blk_exp, n_active, n_valid, w_slot, next_exp, gsrc, sdst, hp, wg, wu, wd)


def _final_body(x_ref, y0_ref, y1_ref, rg_ref, g_ref, o_ref):
    y = _moe_residual(x_ref, y0_ref, y1_ref, rg_ref)
    ms = jnp.mean(y * y, axis=-1, keepdims=True)
    o_ref[...] = (y * lax.rsqrt(ms + EPS)) * g_ref[...]


def _final(x, yk, rg, g):
    tm = 512
    row = pl.BlockSpec((tm, D_MODEL), lambda i: (i, 0))
    return pl.pallas_call(
        _final_body,
        grid=(N_TOK // tm,),
        in_specs=[
            row,
            pl.BlockSpec((None, tm * ROW_TILE, LANES), lambda i: (0, i, 0)),
            pl.BlockSpec((None, tm * ROW_TILE, LANES), lambda i: (1, i, 0)),
            pl.BlockSpec((tm, LANES), lambda i: (i, 0)),
            pl.BlockSpec((1, D_MODEL), lambda i: (0, 0)),
        ],
        out_specs=row,
        out_shape=jax.ShapeDtypeStruct((N_TOK, D_MODEL), F32),
        compiler_params=_params(("parallel",)),
        name="final",
    )(x, yk, yk, rg, g)


def _plan(ri, cnt):
    counts = cnt[0, ROUTE_OFF:ROUTE_OFF + N_EXPERTS].astype(jnp.int32)
    nblk = (counts + TM_MOE - 1) // TM_MOE
    blk_end = jnp.cumsum(nblk)
    blk_start = blk_end - nblk
    pad_start = blk_start * TM_MOE
    n_active = blk_end[-1:]
    blk = jnp.minimum(jnp.arange(N_BLOCKS, dtype=jnp.int32), n_active[0] - 1)
    blk_exp = jnp.sum(blk_end[None, :] <= blk[:, None], axis=1)
    n_valid = jnp.clip(counts[blk_exp] - (blk - blk_start[blk_exp]) * TM_MOE, 0, TM_MOE)
    has_rows = nblk > 0
    w_slot = ((jnp.cumsum(has_rows) - 1) % 2)[blk_exp]
    eids = jnp.arange(N_EXPERTS, dtype=jnp.int32)
    later = (eids[None, :] > eids[:, None]) & has_rows[None, :]
    nxt_e = jnp.min(jnp.where(later, eids[None, :], N_EXPERTS), axis=1)
    next_exp = jnp.where(nxt_e < N_EXPERTS, nxt_e, -1)[blk_exp]
    eid, rank = ri[:, 0:2], ri[:, 2:4]
    onehot = eid[:, :, None] == jnp.arange(N_EXPERTS, dtype=jnp.int32)[None, None, :]
    dest = rank + jnp.sum(jnp.where(onehot, pad_start[None, None, :], 0), axis=-1)
    pad_lo = jnp.concatenate([pad_start + counts, n_active * TM_MOE])
    pad_hi = jnp.concatenate([blk_end * TM_MOE, jnp.full((1,), N_ROWS, jnp.int32)])
    pad_ranges = jnp.stack([pad_lo, pad_hi], axis=1).reshape(-1).astype(jnp.int32)
    rowv = _invert(dest.reshape(-1).astype(jnp.int32), pad_ranges)
    gsrc = (rowv >> 1) * ROW_TILE
    sdst = gsrc + (rowv & 1) * (N_TOK * ROW_TILE)
    return (blk_exp.astype(jnp.int32), n_active.astype(jnp.int32), n_valid.astype(jnp.int32),
            w_slot.astype(jnp.int32), next_exp.astype(jnp.int32), gsrc, sdst)


def kernel(x, mix_norm, w_in, w_fourier, w_spatial, b_spatial, conv_w, w_pool, pool_scale,
           out_norm, w_out, ffn_norm, w_group, b_group, w_router, b_router, w_gate, w_up,
           w_down, final_norm):
    cs, m2 = _fourier_consts()
    tr = 512
    tri = (jnp.arange(tr)[:, None] > jnp.arange(tr)[None, :]).astype(BF16)

    w_in_b = w_in.astype(BF16)
    w_f_b = w_fourier.astype(BF16)
    w_s_b = w_spatial.astype(BF16)
    w_p_b = w_pool.astype(BF16)
    w_o_b = w_out.astype(BF16)
    pad = LANES - N_EXPERT_GROUPS - N_EXPERTS
    wr = jnp.concatenate([w_group, w_router, jnp.zeros((DEPTH, D_MODEL, pad), F32)], axis=-1)
    wr_hi = wr.astype(BF16)
    wr_lo = (wr - wr_hi.astype(F32)).astype(BF16)
    br = jnp.concatenate([b_group, b_router, jnp.zeros((DEPTH, pad), F32)], axis=-1)[:, None, :]
    bsb = jnp.broadcast_to(b_spatial[:, :, :, None], (DEPTH, N_HEADS, CHUNK, HEAD_DIM))
    mix_g = mix_norm[:, None, :]
    ffn_g = ffn_norm[:, None, :]
    out_g = out_norm.reshape(DEPTH, 1, D_MODEL)
    pool_s = pool_scale[:, None, :]

    xt = x.reshape(N_TOK, D_MODEL)
    moe = None
    for l in range(DEPTH):
        if moe is None:
            z = _inproj(l, xt, mix_g, w_in_b)
        else:
            z, xt = _inproj(l, xt, mix_g, w_in_b, moe=moe)
        z3 = z.reshape(BATCH, SEQ, D_IN_PROJ)
        f3 = _fourier(z3, cs, m2)
        x3 = _mixer(l, xt.reshape(BATCH, SEQ, D_MODEL), f3, z3, w_f_b, w_s_b, bsb,
                    conv_w, w_p_b, pool_s, out_g, w_o_b)
        xt = x3.reshape(N_TOK, D_MODEL)
        hp, ri, rg, cnt = _router(l, xt, ffn_g, wr_hi, wr_lo, br, tri)
        plan = _plan(ri, cnt)
        yk = _moe(l, *plan, hp, w_gate, w_up, w_down)
        moe = (yk.reshape(2, N_TOK * ROW_TILE, LANES), rg)
    xt = _final(xt, moe[0], moe[1], final_norm[None, :])
    return xt.reshape(BATCH, SEQ, D_MODEL)
```

```python
import functools

import numpy as np
import jax
import jax.numpy as jnp
from jax import lax
from jax.experimental import pallas as pl
from jax.experimental.pallas import tpu as pltpu

D_MODEL = 2048
BATCH = 4
SEQ = 4096
DEPTH = 4
N_TOK = BATCH * SEQ
D_GROUP = 512
HEAD_DIM = 128
N_HEADS = 4
CHUNK = 128
POOL_WINDOWS = (2, 4, 8, 16)
D_IN_PROJ = 7 * D_GROUP
N_EXPERT_GROUPS = 4
EXPERTS_PER_GROUP = 8
N_EXPERTS = 32
D_EXPERT = 768
EPS = 1e-6

F32 = jnp.float32
BF16 = jnp.bfloat16

FFT_Q = 8
FFT_P = SEQ // FFT_Q
FFT_HEADS = 2
FFT_W = FFT_HEADS * HEAD_DIM
HALO = 16
LANES = 128
ROUTE_OFF = N_EXPERT_GROUPS
ROUTE_ROWS = 4

TM_MOE = 256
N_BLOCKS = (2 * N_TOK) // TM_MOE + N_EXPERTS
N_ROWS = N_BLOCKS * TM_MOE
D_HALF = D_MODEL // 2
ROW_TILE = D_HALF // LANES
ROW_DMA_PRIORITY = 1

VMEM_LIMIT = 56 * 1024 * 1024


def _params(sem, vmem=VMEM_LIMIT):
    return pltpu.CompilerParams(dimension_semantics=sem, vmem_limit_bytes=vmem)


def _unpack_pairs(u):
    lo = pltpu.bitcast(u << 16, F32)
    hi = pltpu.bitcast(u & jnp.uint32(0xFFFF0000), F32)
    return jnp.concatenate([lo, hi], axis=1)


def _pack_pairs(v):
    bits = pltpu.bitcast(v.astype(BF16).astype(F32), jnp.uint32)
    return (bits[:, :D_HALF] >> 16) | (bits[:, D_HALF:] & jnp.uint32(0xFFFF0000))


def _store_row_tiles(ref, packed):
    rows = packed.shape[0]
    for s in range(ROW_TILE):
        ref[pl.ds(s, rows, stride=ROW_TILE), :] = packed[:, s * LANES:(s + 1) * LANES]


def _load_row_tiles(ref, rows):
    return jnp.concatenate(
        [ref[pl.ds(s, rows, stride=ROW_TILE), :] for s in range(ROW_TILE)], axis=1)


def _moe_residual(x_ref, y0_ref, y1_ref, rg_ref):
    rg = rg_ref[...]
    rows = x_ref.shape[0]
    return (x_ref[...] + rg[:, 0:1] * _unpack_pairs(_load_row_tiles(y0_ref, rows))
            + rg[:, 1:2] * _unpack_pairs(_load_row_tiles(y1_ref, rows)))


def _project(x, g_ref, w_ref, z_ref):
    ms = jnp.mean(x * x, axis=-1, keepdims=True)
    h = ((x * lax.rsqrt(ms + EPS)) * g_ref[...]).astype(BF16)
    tn = 512
    for j in range(D_IN_PROJ // tn):
        cs = slice(j * tn, (j + 1) * tn)
        z_ref[:, cs] = jnp.dot(h, w_ref[:, cs], preferred_element_type=F32).astype(BF16)


def _inproj_body(x_ref, g_ref, w_ref, z_ref):
    _project(x_ref[...], g_ref, w_ref, z_ref)


def _inproj_moe_body(x_ref, y0_ref, y1_ref, rg_ref, g_ref, w_ref, z_ref, xo_ref):
    x = _moe_residual(x_ref, y0_ref, y1_ref, rg_ref)
    xo_ref[...] = x
    _project(x, g_ref, w_ref, z_ref)


def _inproj(l, x, g, w, moe=None):
    tm = 512
    row = pl.BlockSpec((tm, D_MODEL), lambda i: (i, 0))
    w_specs = [
        pl.BlockSpec((None, 1, D_MODEL), lambda i: (l, 0, 0)),
        pl.BlockSpec((None, D_MODEL, D_IN_PROJ), lambda i: (l, 0, 0), pipeline_mode=pl.Buffered(1)),
    ]
    z_spec = pl.BlockSpec((tm, D_IN_PROJ), lambda i: (i, 0))
    z_shape = jax.ShapeDtypeStruct((N_TOK, D_IN_PROJ), BF16)
    if moe is None:
        return pl.pallas_call(
            _inproj_body, grid=(N_TOK // tm,), in_specs=[row] + w_specs, out_specs=z_spec,
            out_shape=z_shape, compiler_params=_params(("parallel",)), name="inproj",
        )(x, g, w)
    yk, rg = moe
    return pl.pallas_call(
        _inproj_moe_body,
        grid=(N_TOK // tm,),
        in_specs=[
            row,
            pl.BlockSpec((None, tm * ROW_TILE, LANES), lambda i: (0, i, 0)),
            pl.BlockSpec((None, tm * ROW_TILE, LANES), lambda i: (1, i, 0)),
            pl.BlockSpec((tm, LANES), lambda i: (i, 0)),
        ] + w_specs,
        out_specs=[z_spec, row],
        out_shape=[z_shape, jax.ShapeDtypeStruct((N_TOK, D_MODEL), F32)],
        compiler_params=_params(("parallel",)),
        name="inproj_moe",
    )(x, yk, yk, rg, g, w)


def _fourier_consts():
    c = np.arange(HEAD_DIM)
    ang = 2.0 * np.pi * (np.outer(c, c) % HEAD_DIM) / HEAD_DIM
    cc, sc = np.cos(ang), np.sin(ang)
    scale = 1.0 / np.sqrt(SEQ * HEAD_DIM)
    cs = np.block([[cc, -sc], [sc, cc]]) * scale
    s1 = np.arange(FFT_P)
    m2 = np.zeros((FFT_Q, FFT_P, 2 * FFT_P), np.float64)
    for k2 in range(FFT_Q):
        k = FFT_Q * np.arange(FFT_P) + k2
        th = 2.0 * np.pi * (np.outer(k, s1) % SEQ) / SEQ
        m2[k2, :, :FFT_P] = np.cos(th)
        m2[k2, :, FFT_P:] = np.sin(th)
    return jnp.asarray(cs, BF16), jnp.asarray(m2, BF16)


def _fourier_body(z_ref, cs_ref, m2_ref, f_ref, y_scr, u_scr):
    rc = 64
    r = np.float32(np.sqrt(0.5))

    def chunk(ci, carry):
        r0 = pl.multiple_of(ci * rc, rc)
        zb = [z_ref[pl.ds(s2 * FFT_P + r0, rc), :].astype(F32) for s2 in range(FFT_Q)]
        e0, e1 = zb[0] + zb[4], zb[0] - zb[4]
        e2, e3 = zb[2] + zb[6], zb[2] - zb[6]
        o0, o1 = zb[1] + zb[5], zb[1] - zb[5]
        o2, o3 = zb[3] + zb[7], zb[3] - zb[7]
        p, q = (o1 - o3) * r, (o1 + o3) * r
        ee, oo = e0 + e2, o0 + o2
        ed, od = e0 - e2, o0 - o2
        zero = jnp.zeros_like(e0)
        ys = [(ee + oo, zero), (e1 + p, -e3 - q), (ed, -od), (e1 - p, e3 - q),
              (ee - oo, zero), (e1 - p, q - e3), (ed, od), (e1 + p, e3 + q)]
        for k2 in range(FFT_Q):
            re, im = ys[k2]
            rows = pl.ds(k2 * FFT_P + r0, rc)
            for h in range(FFT_HEADS):
                hs = slice(h * HEAD_DIM, (h + 1) * HEAD_DIM)
                y_scr[rows, 2 * h * HEAD_DIM:(2 * h + 1) * HEAD_DIM] = re[:, hs].astype(BF16)
                y_scr[rows, (2 * h + 1) * HEAD_DIM:(2 * h + 2) * HEAD_DIM] = im[:, hs].astype(BF16)
        return carry

    lax.fori_loop(0, FFT_P // rc, chunk, 0)

    for k2 in range(FFT_Q):
        for h in range(FFT_HEADS):
            yk = y_scr[k2 * FFT_P:(k2 + 1) * FFT_P, 2 * h * HEAD_DIM:(2 * h + 2) * HEAD_DIM]
            ab = jnp.dot(yk, cs_ref[...], preferred_element_type=F32)
            hs = slice(h * HEAD_DIM, (h + 1) * HEAD_DIM)
            u_scr[k2, 0:FFT_P, hs] = ab[:, :HEAD_DIM].astype(BF16)
            u_scr[k2, FFT_P:2 * FFT_P, hs] = ab[:, HEAD_DIM:].astype(BF16)

    for k2 in range(FFT_Q):
        res = jnp.dot(m2_ref[k2], u_scr[k2], preferred_element_type=F32)
        for h in range(FFT_HEADS):
            f_ref[h, pl.ds(k2, FFT_P, stride=FFT_Q), :] = res[:, h * HEAD_DIM:(h + 1) * HEAD_DIM]


def _fourier(z3, cs, m2):
    return pl.pallas_call(
        _fourier_body,
        grid=(BATCH, N_HEADS // FFT_HEADS),
        in_specs=[
            pl.BlockSpec((None, SEQ, FFT_W), lambda b, h: (b, 0, h)),
            pl.BlockSpec((2 * HEAD_DIM, 2 * HEAD_DIM), lambda b, h: (0, 0)),
            pl.BlockSpec((FFT_Q, FFT_P, 2 * FFT_P), lambda b, h: (0, 0, 0)),
        ],
        out_specs=pl.BlockSpec((None, FFT_HEADS, SEQ, HEAD_DIM), lambda b, h: (b, h, 0, 0)),
        out_shape=jax.ShapeDtypeStruct((BATCH, N_HEADS, SEQ, HEAD_DIM), F32),
        scratch_shapes=[
            pltpu.VMEM((SEQ, 2 * FFT_W), BF16),
            pltpu.VMEM((FFT_Q, 2 * FFT_P, FFT_W), BF16),
        ],
        compiler_params=_params(("parallel", "parallel")),
        name="fourier",
    )(z3, cs, m2)


def _gelu(x):
    return 0.5 * x * (1.0 + lax.erf(x * np.float32(np.sqrt(0.5))))


def _mixer_body(x_ref, f_ref, zu_ref, zv_ref, zb_ref, zc_ref, zval_ref, zp_ref,
                zc_prev, zval_prev, zp_prev, zc_next, zval_next, zp_next,
                wf_ref, ws_ref, bs_ref, cw_ref, wp_ref, ps_ref, on_ref, wo_ref,
                fg_ref, wr_ref, wrl_ref, br_ref, tri_ref,
                o_ref, hp_ref, ri_ref, rg_ref, cnt_ref, ext_a, ext_p, carry):
    i = pl.program_id(1)
    ts = x_ref.shape[0]
    keep_prev = (i > 0).astype(F32)
    keep_next = (i < pl.num_programs(1) - 1).astype(F32)
    parts = []

    def norm_store(y, g):
        gs = slice(g * D_GROUP, (g + 1) * D_GROUP)
        ms = jnp.mean(y * y, axis=-1, keepdims=True)
        yn = ((y * lax.rsqrt(ms + EPS)) * on_ref[:, gs]).astype(BF16)
        parts.append(jnp.dot(yn, wo_ref[gs, :], preferred_element_type=F32))

    spec = jnp.concatenate([f_ref[h] for h in range(N_HEADS)], axis=1).astype(BF16)
    norm_store(jnp.dot(spec, wf_ref[...], preferred_element_type=F32), 0)

    gu = _gelu(zu_ref[...].astype(F32))
    gv = _gelu(zv_ref[...].astype(F32)).astype(BF16)
    cols = []
    for h in range(N_HEADS):
        hs = slice(h * HEAD_DIM, (h + 1) * HEAD_DIM)
        rows = []
        for n in range(ts // CHUNK):
            vv = gv[n * CHUNK:(n + 1) * CHUNK, hs]
            rows.append(jnp.dot(ws_ref[h], vv, preferred_element_type=F32) + bs_ref[h])
        cols.append(jnp.concatenate(rows, axis=0))
    norm_store(gu * jnp.concatenate(cols, axis=1), 1)

    def f32(ref):
        return ref[...].astype(F32)

    a = f32(zc_ref) * f32(zval_ref)
    ext_a[0:HALO, :] = f32(zc_prev) * f32(zval_prev) * keep_prev
    ext_a[HALO:HALO + ts, :] = a
    ext_a[HALO + ts:2 * HALO + ts, :] = f32(zc_next) * f32(zval_next) * keep_next
    conv = (cw_ref[0:1, :] * ext_a[HALO - 1:HALO - 1 + ts, :] + cw_ref[1:2, :] * a
            + cw_ref[2:3, :] * ext_a[HALO + 1:HALO + 1 + ts, :])
    norm_store(f32(zb_ref) * conv, 2)

    zp = f32(zp_ref)
    ext_p[0:HALO, :] = f32(zp_prev) * keep_prev
    ext_p[HALO:HALO + ts, :] = zp
    ext_p[HALO + ts:2 * HALO + ts, :] = f32(zp_next) * keep_next
    t = i * ts + lax.broadcasted_iota(jnp.int32, (ts, HEAD_DIM), 0)
    outs = []
    for g, w in enumerate(POOL_WINDOWS):
        gs = slice(g * HEAD_DIM, (g + 1) * HEAD_DIM)
        acc = ext_p[HALO - w // 2:HALO - w // 2 + ts, gs]
        for d in range(-w // 2 + 1, w // 2):
            acc = acc + ext_p[HALO + d:HALO + d + ts, gs]
        cnt = (jnp.minimum(t + w // 2, SEQ) - jnp.maximum(t - w // 2, 0)).astype(F32)
        pg = acc / cnt - zp[:, gs]
        outs.append(jnp.dot(pg.astype(BF16), wp_ref[g], preferred_element_type=F32))
    norm_store(jnp.concatenate(outs, axis=1) * ps_ref[...], 3)

    x_new = x_ref[...] + ((parts[0] + parts[1]) + (parts[2] + parts[3]))
    o_ref[...] = x_new

    @pl.when((pl.program_id(0) == 0) & (i == 0))
    def _():
        carry[...] = jnp.zeros_like(carry)

    _route(x_new, fg_ref, wr_ref, wrl_ref, br_ref, tri_ref, hp_ref, ri_ref, rg_ref, cnt_ref, carry)


def _mixer(l, x3, f3, z3, wf, ws, bsb, cw, wp, ps, on, wo, fg, wr, wrl, br, tri):
    ts = tri.shape[0]
    nb8 = ts // HALO
    last8 = SEQ // HALO - 1
    n_i = SEQ // ts

    def tok(rows, width):
        return pl.BlockSpec((rows, width), lambda b, i: (b * n_i + i, 0))

    def col(j):
        return pl.BlockSpec((None, ts, D_GROUP), lambda b, i, j=j: (b, i, j))

    def prev(j):
        return pl.BlockSpec((None, HALO, D_GROUP),
                            lambda b, i, j=j: (b, jnp.maximum(i * nb8 - 1, 0), j))

    def nxt(j):
        return pl.BlockSpec((None, HALO, D_GROUP),
                            lambda b, i, j=j: (b, jnp.minimum((i + 1) * nb8, last8), j))

    def full(shape, **kw):
        return pl.BlockSpec((None,) + shape, lambda b, i: (l,) + (0,) * len(shape), **kw)

    return pl.pallas_call(
        _mixer_body,
        grid=(BATCH, SEQ // ts),
        in_specs=[
            pl.BlockSpec((None, ts, D_MODEL), lambda b, i: (b, i, 0)),
            pl.BlockSpec((None, N_HEADS, ts, HEAD_DIM), lambda b, i: (b, 0, i, 0)),
            col(1), col(2), col(3), col(4), col(5), col(6),
            prev(4), prev(5), prev(6), nxt(4), nxt(5), nxt(6),
            full((D_GROUP, D_GROUP)), full((N_HEADS, CHUNK, CHUNK)),
            full((N_HEADS, CHUNK, HEAD_DIM)), full((3, D_GROUP)),
            full((4, HEAD_DIM, HEAD_DIM)), full((1, D_GROUP)), full((1, D_MODEL)),
            full((D_MODEL, D_MODEL), pipeline_mode=pl.Buffered(1)),
            full((1, D_MODEL)), full((D_MODEL, LANES)), full((D_MODEL, LANES)), full((1, LANES)),
            pl.BlockSpec((ts, ts), lambda b, i: (0, 0)),
        ],
        out_specs=[
            pl.BlockSpec((None, ts, D_MODEL), lambda b, i: (b, i, 0)),
            tok(ts * ROW_TILE, LANES),
            pl.BlockSpec((ROUTE_ROWS, ts), lambda b, i: (0, b * n_i + i)),
            tok(ts, LANES),
            pl.BlockSpec((1, LANES), lambda b, i: (0, 0)),
        ],
        out_shape=[
            jax.ShapeDtypeStruct((BATCH, SEQ, D_MODEL), F32),
            jax.ShapeDtypeStruct((N_TOK * ROW_TILE, LANES), jnp.uint32),
            jax.ShapeDtypeStruct((ROUTE_ROWS, N_TOK), jnp.int32),
            jax.ShapeDtypeStruct((N_TOK, LANES), F32),
            jax.ShapeDtypeStruct((1, LANES), F32),
        ],
        scratch_shapes=[
            pltpu.VMEM((ts + 2 * HALO, D_GROUP), F32),
            pltpu.VMEM((ts + 2 * HALO, D_GROUP), F32),
            pltpu.VMEM((1, LANES), F32),
        ],
        compiler_params=_params(("arbitrary", "arbitrary")),
        name="mixer",
    )(x3, f3, z3, z3, z3, z3, z3, z3, z3, z3, z3, z3, z3, z3, wf, ws, bsb, cw, wp, ps, on, wo,
      fg, wr, wrl, br, tri)


def _route(x, g_ref, wr_ref, wrl_ref, br_ref, tri_ref, hp_ref, ri_ref, rg_ref, cnt_ref, carry):
    tr = x.shape[0]
    ms = jnp.mean(x * x, axis=-1, keepdims=True)
    h = (x * lax.rsqrt(ms + EPS)) * g_ref[...]

    _store_row_tiles(hp_ref, _pack_pairs(h))

    h_hi = h.astype(BF16)
    h_lo = (h - h_hi.astype(F32)).astype(BF16)
    logits = (jnp.dot(h_hi, wr_ref[...], preferred_element_type=F32)
              + jnp.dot(h_lo, wr_ref[...], preferred_element_type=F32)
              + jnp.dot(h_hi, wrl_ref[...], preferred_element_type=F32)) + br_ref[...]
    lane = lax.broadcasted_iota(jnp.int32, (tr, LANES), 1)
    neg = np.float32(-np.inf)

    def first_argmax(v, vmax):
        return jnp.min(jnp.where(v == vmax, lane, LANES), axis=-1, keepdims=True)

    mg = jnp.where(lane < N_EXPERT_GROUPS, logits, neg)
    gmax = jnp.max(mg, axis=-1, keepdims=True)
    grp = first_argmax(mg, gmax)
    p_grp = 1.0 / jnp.sum(jnp.exp(mg - gmax), axis=-1, keepdims=True)

    lo = ROUTE_OFF + EXPERTS_PER_GROUP * grp
    le = jnp.where(lane >= lo, jnp.where(lane < lo + EXPERTS_PER_GROUP, logits, neg), neg)
    emax = jnp.max(le, axis=-1, keepdims=True)
    i1 = first_argmax(le, emax)
    le2 = jnp.where(lane == i1, neg, le)
    emax2 = jnp.max(le2, axis=-1, keepdims=True)
    i2 = first_argmax(le2, emax2)
    e2 = jnp.exp(emax2 - emax)
    gate1 = p_grp / (1.0 + e2)
    gate2 = p_grp * e2 / (1.0 + e2)

    is1, is2 = lane == i1, lane == i2
    oh = jnp.where(is1, 1.0, jnp.where(is2, 1.0, 0.0))
    pref = jnp.dot(tri_ref[...], oh.astype(BF16), preferred_element_type=F32) + carry[...]
    r1 = jnp.sum(jnp.where(is1, pref, 0.0), axis=-1, keepdims=True).astype(jnp.int32)
    r2 = jnp.sum(jnp.where(is2, pref, 0.0), axis=-1, keepdims=True).astype(jnp.int32)
    carry[...] = carry[...] + jnp.sum(oh, axis=0, keepdims=True)
    cnt_ref[...] = carry[...]

    ri = jnp.where(lane == 0, i1 - ROUTE_OFF,
                   jnp.where(lane == 1, i2 - ROUTE_OFF,
                             jnp.where(lane == 2, r1, jnp.where(lane == 3, r2, 0))))
    ri_ref[...] = ri.T[0:ri_ref.shape[0], :]
    rg_ref[...] = jnp.where(lane == 0, gate1, jnp.where(lane == 1, gate2, 0.0))


def _invert_body(dest_ref, pad_ref, rowv_ref):
    def fill(j, c):
        def zero(r, c2):
            rowv_ref[r] = 0
            return c2

        lax.fori_loop(pad_ref[2 * j], pad_ref[2 * j + 1], zero, 0)
        return c

    lax.fori_loop(0, N_EXPERTS + 1, fill, 0)

    unroll = 16

    def body(j, c):
        rows = [dest_ref[j * unroll + u] for u in range(unroll)]
        for u in range(unroll):
            rowv_ref[rows[u]] = j * unroll + u
        return c

    lax.fori_loop(0, 2 * N_TOK // unroll, body, 0)


def _invert(dest, pad_ranges):
    return pl.pallas_call(
        _invert_body,
        grid_spec=pltpu.PrefetchScalarGridSpec(
            num_scalar_prefetch=2,
            grid=(1,),
            in_specs=[],
            out_specs=pl.BlockSpec(memory_space=pltpu.SMEM),
        ),
        out_shape=jax.ShapeDtypeStruct((N_ROWS,), jnp.int32),
        compiler_params=_params(("arbitrary",)),
        name="invert",
    )(dest, pad_ranges)


def _moe_body(be_ref, na_ref, nv_ref, par_ref, nxe_ref, gs_ref, sd_ref, hp_ref, wg_hbm, wu_hbm,
              wd_hbm, yk_ref, wg32, wu32, wd32, wg_ref, wu_ref, wd_ref, xbuf, ybuf, yacc, gsem, ssem,
              wsem, *, layer):
    i = pl.program_id(0)
    na = na_ref[0]

    def weight_copies(e, p):
        return [pltpu.make_async_copy(src.at[layer, e], dst.at[p], wsem.at[p])
                for src, dst in ((wg_hbm, wg32), (wu_hbm, wu32), (wd_hbm, wd32))]
    slot = lax.rem(i, 2)
    other = 1 - slot
    blk_rows = TM_MOE * ROW_TILE

    def tile_rows(r):
        start = r * ROW_TILE
        return pl.ds(start if isinstance(r, int) else pl.multiple_of(start, ROW_TILE), ROW_TILE)

    def gather_copy(blk, s, r, src=None):
        if src is None:
            src = gs_ref[blk * TM_MOE + r]
        return pltpu.make_async_copy(hp_ref.at[pl.ds(pl.multiple_of(src, ROW_TILE), ROW_TILE), :],
                                     xbuf.at[s, tile_rows(r), :], gsem.at[s])

    def scatter_copy(blk, s, r, dst=None):
        if dst is None:
            dst = sd_ref[blk * TM_MOE + r]
        return pltpu.make_async_copy(ybuf.at[s, tile_rows(r), :],
                                     yk_ref.at[pl.ds(pl.multiple_of(dst, ROW_TILE), ROW_TILE), :],
                                     ssem.at[s])

    def wait_gather(s):
        pltpu.make_async_copy(hp_ref.at[pl.ds(0, blk_rows), :], xbuf.at[s], gsem.at[s]).wait()

    def wait_scatter(blk, s):
        n = nv_ref[blk]

        @pl.when(n == TM_MOE)
        def _():
            pltpu.make_async_copy(ybuf.at[s], yk_ref.at[pl.ds(0, blk_rows), :], ssem.at[s]).wait()

        @pl.when(n < TM_MOE)
        def _():
            lax.fori_loop(0, n, lambda r, c: (scatter_copy(blk, s, r).wait(), c)[1], 0)

    yslot = lax.rem(i, 3)
    yprev = lax.rem(i + 2, 3)
    nv_prev = nv_ref[jnp.maximum(i - 1, 0)]
    prev_full = (i >= 1) & (nv_prev == TM_MOE)

    def start(copy, priority=ROW_DMA_PRIORITY):
        copy.start(priority=priority)

    @pl.when(i == 0)
    def _():
        for c in weight_copies(be_ref[0], 0):
            c.start()
        lax.fori_loop(0, TM_MOE, lambda r, c: (start(gather_copy(0, 0, r)), c)[1], 0)

    @pl.when((i < na) & ((i == 0) | (be_ref[i] != be_ref[jnp.maximum(i - 1, 0)])))
    def _():
        p = par_ref[i]
        for c in weight_copies(be_ref[i], p):
            c.wait()

        @pl.when(nxe_ref[i] >= 0)
        def _():
            for c in weight_copies(nxe_ref[i], 1 - p):
                c.start()

        wg_ref[...] = wg32[p].astype(BF16)
        wu_ref[...] = wu32[p].astype(BF16)
        wd_ref[...] = wd32[p].astype(BF16)

    @pl.when((i >= 3) & (i < na))
    def _():
        wait_scatter(i - 3, yslot)

    @pl.when((i >= 1) & (i < na) & (nv_prev < TM_MOE))
    def _():
        lax.fori_loop(0, nv_prev, lambda r, c: (start(scatter_copy(i - 1, yprev, r)), c)[1], 0)

    def block_body(scatter_prev):
        wait_gather(slot)
        xu = _load_row_tiles(xbuf.at[slot], TM_MOE)
        lo = pltpu.bitcast(xu << 16, F32).astype(BF16)
        hi = pltpu.bitcast(xu & jnp.uint32(0xFFFF0000), F32).astype(BF16)
        nxt = jnp.minimum(i + 1, na - 1)
        for r in range(TM_MOE):
            start(gather_copy(nxt, other, r), priority=r % 2)
            if scatter_prev:
                start(scatter_copy(i - 1, yprev, r), priority=(r + 1) % 2)
        fc = 256
        for c in range(D_EXPERT // fc):
            cs = slice(c * fc, (c + 1) * fc)
            g = (jnp.dot(lo, wg_ref[0:D_HALF, cs], preferred_element_type=F32)
                 + jnp.dot(hi, wg_ref[D_HALF:D_MODEL, cs], preferred_element_type=F32))
            u = (jnp.dot(lo, wu_ref[0:D_HALF, cs], preferred_element_type=F32)
                 + jnp.dot(hi, wu_ref[D_HALF:D_MODEL, cs], preferred_element_type=F32))
            a = (g * jax.nn.sigmoid(g) * u).astype(BF16)
            part = jnp.dot(a, wd_ref[cs, :], preferred_element_type=F32)
            if c == 0:
                yacc[...] = part
            else:
                yacc[...] += part
        _store_row_tiles(ybuf.at[yslot], _pack_pairs(yacc[...]))

    @pl.when((i < na) & prev_full)
    def _():
        block_body(True)

    @pl.when((i < na) & jnp.logical_not(prev_full))
    def _():
        block_body(False)

    @pl.when(i == na - 1)
    def _():
        lax.fori_loop(0, nv_ref[i], lambda r, c: (start(scatter_copy(i, yslot, r)), c)[1], 0)
        wait_gather(other)
        wait_scatter(i, yslot)

        @pl.when(i >= 1)
        def _():
            wait_scatter(i - 1, yprev)

        @pl.when(i >= 2)
        def _():
            wait_scatter(i - 2, lax.rem(i + 1, 3))


def _moe(l, blk_exp, n_active, n_valid, w_slot, next_exp, gsrc, sdst, hp, wg, wu, wd):
    return pl.pallas_call(
        functools.partial(_moe_body, layer=l),
        grid_spec=pltpu.PrefetchScalarGridSpec(
            num_scalar_prefetch=7,
            grid=(N_BLOCKS,),
            in_specs=[pl.BlockSpec(memory_space=pl.ANY)] * 4,
            out_specs=pl.BlockSpec(memory_space=pl.ANY),
            scratch_shapes=[
                pltpu.VMEM((2, D_MODEL, D_EXPERT), F32),
                pltpu.VMEM((2, D_MODEL, D_EXPERT), F32),
                pltpu.VMEM((2, D_EXPERT, D_MODEL), F32),
                pltpu.VMEM((D_MODEL, D_EXPERT), BF16),
                pltpu.VMEM((D_MODEL, D_EXPERT), BF16),
                pltpu.VMEM((D_EXPERT, D_MODEL), BF16),
                pltpu.VMEM((2, TM_MOE * ROW_TILE, LANES), jnp.uint32),
                pltpu.VMEM((3, TM_MOE * ROW_TILE, LANES), jnp.uint32),
                pltpu.VMEM((TM_MOE, D_MODEL), F32),
                pltpu.SemaphoreType.DMA((2,)),
                pltpu.SemaphoreType.DMA((3,)),
                pltpu.SemaphoreType.DMA((2,)),
            ],
        ),
        out_shape=jax.ShapeDtypeStruct((2 * N_TOK * ROW_TILE, LANES), jnp.uint32),
        compiler_params=_params(("arbitrary",), vmem=60 * 1024 * 1024),
        name="moe",
    )(blk_exp, n_active, n_valid, w_slot, next_exp, gsrc, sdst, hp, wg, wu, wd)


def _final_body(x_ref, y0_ref, y1_ref, rg_ref, g_ref, o_ref):
    y = _moe_residual(x_ref, y0_ref, y1_ref, rg_ref)
    ms = jnp.mean(y * y, axis=-1, keepdims=True)
    o_ref[...] = (y * lax.rsqrt(ms + EPS)) * g_ref[...]


def _final(x, yk, rg, g):
    tm = 512
    row = pl.BlockSpec((tm, D_MODEL), lambda i: (i, 0))
    return pl.pallas_call(
        _final_body,
        grid=(N_TOK // tm,),
        in_specs=[
            row,
            pl.BlockSpec((None, tm * ROW_TILE, LANES), lambda i: (0, i, 0)),
            pl.BlockSpec((None, tm * ROW_TILE, LANES), lambda i: (1, i, 0)),
            pl.BlockSpec((tm, LANES), lambda i: (i, 0)),
            pl.BlockSpec((1, D_MODEL), lambda i: (0, 0)),
        ],
        out_specs=row,
        out_shape=jax.ShapeDtypeStruct((N_TOK, D_MODEL), F32),
        compiler_params=_params(("parallel",)),
        name="final",
    )(x, yk, yk, rg, g)


def _plan(ri, cnt):
    counts = cnt[0, ROUTE_OFF:ROUTE_OFF + N_EXPERTS].astype(jnp.int32)
    nblk = (counts + TM_MOE - 1) // TM_MOE
    blk_end = jnp.cumsum(nblk)
    blk_start = blk_end - nblk
    pad_start = blk_start * TM_MOE
    n_active = blk_end[-1:]
    blk = jnp.minimum(jnp.arange(N_BLOCKS, dtype=jnp.int32), n_active[0] - 1)
    blk_exp = jnp.sum(blk_end[None, :] <= blk[:, None], axis=1)
    n_valid = jnp.clip(counts[blk_exp] - (blk - blk_start[blk_exp]) * TM_MOE, 0, TM_MOE)
    has_rows = nblk > 0
    w_slot = ((jnp.cumsum(has_rows) - 1) % 2)[blk_exp]
    eids = jnp.arange(N_EXPERTS, dtype=jnp.int32)
    later = (eids[None, :] > eids[:, None]) & has_rows[None, :]
    nxt_e = jnp.min(jnp.where(later, eids[None, :], N_EXPERTS), axis=1)
    next_exp = jnp.where(nxt_e < N_EXPERTS, nxt_e, -1)[blk_exp]
    pad_lo = jnp.concatenate([pad_start + counts, n_active * TM_MOE])
    pad_hi = jnp.concatenate([blk_end * TM_MOE, jnp.full((1,), N_ROWS, jnp.int32)])
    pad_ranges = jnp.stack([pad_lo, pad_hi], axis=1).reshape(-1).astype(jnp.int32)
    onehot = ri[0:2, :, None] == eids[None, None, :]
    dest = ri[2:4] + jnp.sum(jnp.where(onehot, pad_start[None, None, :], 0), axis=-1)
    rowv = _invert(dest.reshape(-1).astype(jnp.int32), pad_ranges)
    gsrc = (rowv & (N_TOK - 1)) * ROW_TILE
    sdst = rowv * ROW_TILE
    return (blk_exp.astype(jnp.int32), n_active.astype(jnp.int32), n_valid.astype(jnp.int32),
            w_slot.astype(jnp.int32), next_exp.astype(jnp.int32), gsrc, sdst)


def kernel(x, mix_norm, w_in, w_fourier, w_spatial, b_spatial, conv_w, w_pool, pool_scale,
           out_norm, w_out, ffn_norm, w_group, b_group, w_router, b_router, w_gate, w_up,
           w_down, final_norm):
    cs, m2 = _fourier_consts()
    tr = 512
    tri = (jnp.arange(tr)[:, None] > jnp.arange(tr)[None, :]).astype(BF16)

    w_in_b = w_in.astype(BF16)
    w_f_b = w_fourier.astype(BF16)
    w_s_b = w_spatial.astype(BF16)
    w_p_b = w_pool.astype(BF16)
    w_o_b = w_out.astype(BF16)
    pad = LANES - N_EXPERT_GROUPS - N_EXPERTS
    wr = jnp.concatenate([w_group, w_router, jnp.zeros((DEPTH, D_MODEL, pad), F32)], axis=-1)
    wr_hi = wr.astype(BF16)
    wr_lo = (wr - wr_hi.astype(F32)).astype(BF16)
    br = jnp.concatenate([b_group, b_router, jnp.zeros((DEPTH, pad), F32)], axis=-1)[:, None, :]
    bsb = jnp.broadcast_to(b_spatial[:, :, :, None], (DEPTH, N_HEADS, CHUNK, HEAD_DIM))
    mix_g = mix_norm[:, None, :]
    ffn_g = ffn_norm[:, None, :]
    out_g = out_norm.reshape(DEPTH, 1, D_MODEL)
    pool_s = pool_scale[:, None, :]

    xt = x.reshape(N_TOK, D_MODEL)
    moe = None
    for l in range(DEPTH):
        if moe is None:
            z = _inproj(l, xt, mix_g, w_in_b)
        else:
            z, xt = _inproj(l, xt, mix_g, w_in_b, moe=moe)
        z3 = z.reshape(BATCH, SEQ, D_IN_PROJ)
        f3 = _fourier(z3, cs, m2)
        x3, hp, ri, rg, cnt = _mixer(l, xt.reshape(BATCH, SEQ, D_MODEL), f3, z3, w_f_b, w_s_b, bsb,
                                     conv_w, w_p_b, pool_s, out_g, w_o_b, ffn_g, wr_hi, wr_lo, br, tri)
        xt = x3.reshape(N_TOK, D_MODEL)
        plan = _plan(ri, cnt)
        yk = _moe(l, *plan, hp, w_gate, w_up, w_down)
        moe = (yk.reshape(2, N_TOK * ROW_TILE, LANES), rg)
    xt = _final(xt, moe[0], moe[1], final_norm[None, :])
    return xt.reshape(BATCH, SEQ, D_MODEL)
```

```python
import functools

import numpy as np
import jax
import jax.numpy as jnp
from jax import lax
from jax.experimental import pallas as pl
from jax.experimental.pallas import tpu as pltpu

D_MODEL = 2048
BATCH = 4
SEQ = 4096
DEPTH = 4
N_TOK = BATCH * SEQ
D_GROUP = 512
HEAD_DIM = 128
N_HEADS = 4
CHUNK = 128
POOL_WINDOWS = (2, 4, 8, 16)
D_IN_PROJ = 7 * D_GROUP
N_EXPERT_GROUPS = 4
EXPERTS_PER_GROUP = 8
N_EXPERTS = 32
D_EXPERT = 768
EPS = 1e-6

F32 = jnp.float32
BF16 = jnp.bfloat16

FFT_Q = 8
FFT_P = SEQ // FFT_Q
FFT_HEADS = 2
FFT_W = FFT_HEADS * HEAD_DIM
HALO = 16
LANES = 128
ROUTE_OFF = N_EXPERT_GROUPS
ROUTE_ROWS = 4
MIX_ROWS = 256

TM_MOE = 256
N_BLOCKS = (2 * N_TOK) // TM_MOE + N_EXPERTS
N_ROWS = N_BLOCKS * TM_MOE
PLAN_EXP, PLAN_VALID, PLAN_SLOT, PLAN_NEXT = (k * N_BLOCKS for k in range(4))
PLAN_ACTIVE = 4 * N_BLOCKS
PLAN_LEN = PLAN_ACTIVE + 1
D_HALF = D_MODEL // 2
ROW_TILE = D_HALF // LANES
ROW_DMA_PRIORITY = 1

VMEM_LIMIT = 56 * 1024 * 1024


def _params(sem, vmem=VMEM_LIMIT):
    return pltpu.CompilerParams(dimension_semantics=sem, vmem_limit_bytes=vmem)


def _unpack_pairs(u):
    lo = pltpu.bitcast(u << 16, F32)
    hi = pltpu.bitcast(u & jnp.uint32(0xFFFF0000), F32)
    return jnp.concatenate([lo, hi], axis=1)


def _pack_pairs(v):
    bits = pltpu.bitcast(v.astype(BF16).astype(F32), jnp.uint32)
    return (bits[:, :D_HALF] >> 16) | (bits[:, D_HALF:] & jnp.uint32(0xFFFF0000))


def _store_row_tiles(ref, packed):
    rows = packed.shape[0]
    for s in range(ROW_TILE):
        ref[pl.ds(s, rows, stride=ROW_TILE), :] = packed[:, s * LANES:(s + 1) * LANES]


def _load_row_tiles(ref, rows):
    return jnp.concatenate(
        [ref[pl.ds(s, rows, stride=ROW_TILE), :] for s in range(ROW_TILE)], axis=1)


def _moe_residual(x_ref, y0_ref, y1_ref, rg_ref):
    rg = rg_ref[...]
    rows = x_ref.shape[0]
    return (x_ref[...] + rg[:, 0:1] * _unpack_pairs(_load_row_tiles(y0_ref, rows))
            + rg[:, 1:2] * _unpack_pairs(_load_row_tiles(y1_ref, rows)))


def _project(x, g_ref, w_ref, z_ref):
    ms = jnp.mean(x * x, axis=-1, keepdims=True)
    h = ((x * lax.rsqrt(ms + EPS)) * g_ref[...]).astype(BF16)
    tn = 512
    for j in range(D_IN_PROJ // tn):
        cs = slice(j * tn, (j + 1) * tn)
        z_ref[:, cs] = jnp.dot(h, w_ref[:, cs], preferred_element_type=F32).astype(BF16)


def _inproj_body(x_ref, g_ref, w_ref, z_ref):
    _project(x_ref[...], g_ref, w_ref, z_ref)


def _inproj_moe_body(x_ref, y0_ref, y1_ref, rg_ref, g_ref, w_ref, z_ref, xo_ref):
    x = _moe_residual(x_ref, y0_ref, y1_ref, rg_ref)
    xo_ref[...] = x
    _project(x, g_ref, w_ref, z_ref)


def _inproj(l, x, g, w, moe=None):
    tm = 512
    row = pl.BlockSpec((tm, D_MODEL), lambda i: (i, 0))
    w_specs = [
        pl.BlockSpec((None, 1, D_MODEL), lambda i: (l, 0, 0)),
        pl.BlockSpec((None, D_MODEL, D_IN_PROJ), lambda i: (l, 0, 0), pipeline_mode=pl.Buffered(1)),
    ]
    z_spec = pl.BlockSpec((tm, D_IN_PROJ), lambda i: (i, 0))
    z_shape = jax.ShapeDtypeStruct((N_TOK, D_IN_PROJ), BF16)
    if moe is None:
        return pl.pallas_call(
            _inproj_body, grid=(N_TOK // tm,), in_specs=[row] + w_specs, out_specs=z_spec,
            out_shape=z_shape, compiler_params=_params(("parallel",)), name="inproj",
        )(x, g, w)
    yk, rg = moe
    return pl.pallas_call(
        _inproj_moe_body,
        grid=(N_TOK // tm,),
        in_specs=[
            row,
            pl.BlockSpec((None, tm * ROW_TILE, LANES), lambda i: (0, i, 0)),
            pl.BlockSpec((None, tm * ROW_TILE, LANES), lambda i: (1, i, 0)),
            pl.BlockSpec((tm, LANES), lambda i: (i, 0)),
        ] + w_specs,
        out_specs=[z_spec, row],
        out_shape=[z_shape, jax.ShapeDtypeStruct((N_TOK, D_MODEL), F32)],
        compiler_params=_params(("parallel",)),
        name="inproj_moe",
    )(x, yk, yk, rg, g, w)


def _fourier_consts():
    c = np.arange(HEAD_DIM)
    ang = 2.0 * np.pi * (np.outer(c, c) % HEAD_DIM) / HEAD_DIM
    cc, sc = np.cos(ang), np.sin(ang)
    scale = 1.0 / np.sqrt(SEQ * HEAD_DIM)
    cs = np.block([[cc, -sc], [sc, cc]]) * scale
    s1 = np.arange(FFT_P)
    m2 = np.zeros((FFT_Q, FFT_P, 2 * FFT_P), np.float64)
    for k2 in range(FFT_Q):
        k = FFT_Q * np.arange(FFT_P) + k2
        th = 2.0 * np.pi * (np.outer(k, s1) % SEQ) / SEQ
        m2[k2, :, :FFT_P] = np.cos(th)
        m2[k2, :, FFT_P:] = np.sin(th)
    return jnp.asarray(cs, BF16), jnp.asarray(m2, BF16)


def _fourier_body(z_ref, cs_ref, m2_ref, f_ref, y_scr, u_scr):
    rc = 64
    r = np.float32(np.sqrt(0.5))

    def chunk(ci, carry):
        r0 = pl.multiple_of(ci * rc, rc)
        zb = [z_ref[pl.ds(s2 * FFT_P + r0, rc), :].astype(F32) for s2 in range(FFT_Q)]
        e0, e1 = zb[0] + zb[4], zb[0] - zb[4]
        e2, e3 = zb[2] + zb[6], zb[2] - zb[6]
        o0, o1 = zb[1] + zb[5], zb[1] - zb[5]
        o2, o3 = zb[3] + zb[7], zb[3] - zb[7]
        p, q = (o1 - o3) * r, (o1 + o3) * r
        ee, oo = e0 + e2, o0 + o2
        ed, od = e0 - e2, o0 - o2
        zero = jnp.zeros_like(e0)
        ys = [(ee + oo, zero), (e1 + p, -e3 - q), (ed, -od), (e1 - p, e3 - q),
              (ee - oo, zero), (e1 - p, q - e3), (ed, od), (e1 + p, e3 + q)]
        for k2 in range(FFT_Q):
            re, im = ys[k2]
            rows = pl.ds(k2 * FFT_P + r0, rc)
            for h in range(FFT_HEADS):
                hs = slice(h * HEAD_DIM, (h + 1) * HEAD_DIM)
                y_scr[rows, 2 * h * HEAD_DIM:(2 * h + 1) * HEAD_DIM] = re[:, hs].astype(BF16)
                y_scr[rows, (2 * h + 1) * HEAD_DIM:(2 * h + 2) * HEAD_DIM] = im[:, hs].astype(BF16)
        return carry

    lax.fori_loop(0, FFT_P // rc, chunk, 0)

    for k2 in range(FFT_Q):
        for h in range(FFT_HEADS):
            yk = y_scr[k2 * FFT_P:(k2 + 1) * FFT_P, 2 * h * HEAD_DIM:(2 * h + 2) * HEAD_DIM]
            ab = jnp.dot(yk, cs_ref[...], preferred_element_type=F32)
            hs = slice(h * HEAD_DIM, (h + 1) * HEAD_DIM)
            u_scr[k2, 0:FFT_P, hs] = ab[:, :HEAD_DIM].astype(BF16)
            u_scr[k2, FFT_P:2 * FFT_P, hs] = ab[:, HEAD_DIM:].astype(BF16)

    for k2 in range(FFT_Q):
        res = jnp.dot(m2_ref[k2], u_scr[k2], preferred_element_type=F32)
        for h in range(FFT_HEADS):
            f_ref[h, pl.ds(k2, FFT_P, stride=FFT_Q), :] = res[:, h * HEAD_DIM:(h + 1) * HEAD_DIM]


def _fourier(z3, cs, m2):
    return pl.pallas_call(
        _fourier_body,
        grid=(BATCH, N_HEADS // FFT_HEADS),
        in_specs=[
            pl.BlockSpec((None, SEQ, FFT_W), lambda b, h: (b, 0, h)),
            pl.BlockSpec((2 * HEAD_DIM, 2 * HEAD_DIM), lambda b, h: (0, 0)),
            pl.BlockSpec((FFT_Q, FFT_P, 2 * FFT_P), lambda b, h: (0, 0, 0)),
        ],
        out_specs=pl.BlockSpec((None, FFT_HEADS, SEQ, HEAD_DIM), lambda b, h: (b, h, 0, 0)),
        out_shape=jax.ShapeDtypeStruct((BATCH, N_HEADS, SEQ, HEAD_DIM), F32),
        scratch_shapes=[
            pltpu.VMEM((SEQ, 2 * FFT_W), BF16),
            pltpu.VMEM((FFT_Q, 2 * FFT_P, FFT_W), BF16),
        ],
        compiler_params=_params(("parallel", "parallel")),
        name="fourier",
    )(z3, cs, m2)


def _gelu(x):
    return 0.5 * x * (1.0 + lax.erf(x * np.float32(np.sqrt(0.5))))


def _mixer_body(x_ref, f_ref, zu_ref, zv_ref, zb_ref, zc_ref, zval_ref, zp_ref,
                zc_prev, zval_prev, zp_prev, zc_next, zval_next, zp_next,
                wf_ref, ws_ref, bs_ref, cw_ref, wp_ref, ps_ref, on_ref, wo_ref,
                fg_ref, wr_ref, wrl_ref, br_ref, tri_ref,
                o_ref, hp_ref, ri_ref, rg_ref, cnt_ref, ext_a, ext_p, carry):
    i = pl.program_id(1)
    ts = x_ref.shape[0]
    keep_prev = (i > 0).astype(F32)
    keep_next = (i < pl.num_programs(1) - 1).astype(F32)

    def f32(ref):
        return ref[...].astype(F32)

    ext_a[0:HALO, :] = f32(zc_prev) * f32(zval_prev) * keep_prev
    ext_a[HALO:HALO + ts, :] = f32(zc_ref) * f32(zval_ref)
    ext_a[HALO + ts:2 * HALO + ts, :] = f32(zc_next) * f32(zval_next) * keep_next
    ext_p[0:HALO, :] = f32(zp_prev) * keep_prev
    ext_p[HALO:HALO + ts, :] = f32(zp_ref)
    ext_p[HALO + ts:2 * HALO + ts, :] = f32(zp_next) * keep_next

    def mix_rows(r0, nr):
        rs = slice(r0, r0 + nr)
        parts = []

        def norm_project(y, g):
            gs = slice(g * D_GROUP, (g + 1) * D_GROUP)
            ms = jnp.mean(y * y, axis=-1, keepdims=True)
            yn = ((y * lax.rsqrt(ms + EPS)) * on_ref[:, gs]).astype(BF16)
            parts.append(jnp.dot(yn, wo_ref[gs, :], preferred_element_type=F32))

        spec = jnp.concatenate([f_ref[h, rs, :] for h in range(N_HEADS)], axis=1).astype(BF16)
        norm_project(jnp.dot(spec, wf_ref[...], preferred_element_type=F32), 0)

        gu = _gelu(zu_ref[rs, :].astype(F32))
        gv = _gelu(zv_ref[rs, :].astype(F32)).astype(BF16)
        cols = []
        for h in range(N_HEADS):
            hs = slice(h * HEAD_DIM, (h + 1) * HEAD_DIM)
            rows = []
            for n in range(nr // CHUNK):
                vv = gv[n * CHUNK:(n + 1) * CHUNK, hs]
                rows.append(jnp.dot(ws_ref[h], vv, preferred_element_type=F32) + bs_ref[h])
            cols.append(jnp.concatenate(rows, axis=0))
        norm_project(gu * jnp.concatenate(cols, axis=1), 1)

        conv = (cw_ref[0:1, :] * ext_a[HALO - 1 + r0:HALO - 1 + r0 + nr, :]
                + cw_ref[1:2, :] * ext_a[HALO + r0:HALO + r0 + nr, :]
                + cw_ref[2:3, :] * ext_a[HALO + 1 + r0:HALO + 1 + r0 + nr, :])
        norm_project(zb_ref[rs, :].astype(F32) * conv, 2)

        t = i * ts + r0 + lax.broadcasted_iota(jnp.int32, (nr, HEAD_DIM), 0)
        outs = []
        for g, w in enumerate(POOL_WINDOWS):
            gs = slice(g * HEAD_DIM, (g + 1) * HEAD_DIM)
            acc = ext_p[HALO - w // 2 + r0:HALO - w // 2 + r0 + nr, gs]
            for d in range(-w // 2 + 1, w // 2):
                acc = acc + ext_p[HALO + d + r0:HALO + d + r0 + nr, gs]
            cnt = (jnp.minimum(t + w // 2, SEQ) - jnp.maximum(t - w // 2, 0)).astype(F32)
            pg = acc / cnt - ext_p[HALO + r0:HALO + r0 + nr, gs]
            outs.append(jnp.dot(pg.astype(BF16), wp_ref[g], preferred_element_type=F32))
        norm_project(jnp.concatenate(outs, axis=1) * ps_ref[...], 3)

        o_ref[rs, :] = x_ref[rs, :] + ((parts[0] + parts[1]) + (parts[2] + parts[3]))

    for r0 in range(0, ts, MIX_ROWS):
        mix_rows(r0, MIX_ROWS)

    @pl.when((pl.program_id(0) == 0) & (i == 0))
    def _():
        carry[...] = jnp.zeros_like(carry)

    _route(o_ref[...], fg_ref, wr_ref, wrl_ref, br_ref, tri_ref, hp_ref, ri_ref, rg_ref, cnt_ref,
           carry)


def _mixer(l, x3, f3, z3, wf, ws, bsb, cw, wp, ps, on, wo, fg, wr, wrl, br, tri):
    ts = tri.shape[0]
    nb8 = ts // HALO
    last8 = SEQ // HALO - 1
    n_i = SEQ // ts

    def tok(rows, width):
        return pl.BlockSpec((rows, width), lambda b, i: (b * n_i + i, 0))

    def col(j):
        return pl.BlockSpec((None, ts, D_GROUP), lambda b, i, j=j: (b, i, j))

    def prev(j):
        return pl.BlockSpec((None, HALO, D_GROUP),
                            lambda b, i, j=j: (b, jnp.maximum(i * nb8 - 1, 0), j))

    def nxt(j):
        return pl.BlockSpec((None, HALO, D_GROUP),
                            lambda b, i, j=j: (b, jnp.minimum((i + 1) * nb8, last8), j))

    def full(shape, **kw):
        return pl.BlockSpec((None,) + shape, lambda b, i: (l,) + (0,) * len(shape), **kw)

    return pl.pallas_call(
        _mixer_body,
        grid=(BATCH, SEQ // ts),
        in_specs=[
            pl.BlockSpec((None, ts, D_MODEL), lambda b, i: (b, i, 0)),
            pl.BlockSpec((None, N_HEADS, ts, HEAD_DIM), lambda b, i: (b, 0, i, 0)),
            col(1), col(2), col(3), col(4), col(5), col(6),
            prev(4), prev(5), prev(6), nxt(4), nxt(5), nxt(6),
            full((D_GROUP, D_GROUP)), full((N_HEADS, CHUNK, CHUNK)),
            full((N_HEADS, CHUNK, HEAD_DIM)), full((3, D_GROUP)),
            full((4, HEAD_DIM, HEAD_DIM)), full((1, D_GROUP)), full((1, D_MODEL)),
            full((D_MODEL, D_MODEL), pipeline_mode=pl.Buffered(1)),
            full((1, D_MODEL)), full((D_MODEL, LANES)), full((D_MODEL, LANES)), full((1, LANES)),
            pl.BlockSpec((ts, ts), lambda b, i: (0, 0)),
        ],
        out_specs=[
            pl.BlockSpec((None, ts, D_MODEL), lambda b, i: (b, i, 0)),
            tok(ts * ROW_TILE, LANES),
            pl.BlockSpec((ROUTE_ROWS, ts), lambda b, i: (0, b * n_i + i)),
            tok(ts, LANES),
            pl.BlockSpec((1, LANES), lambda b, i: (0, 0)),
        ],
        out_shape=[
            jax.ShapeDtypeStruct((BATCH, SEQ, D_MODEL), F32),
            jax.ShapeDtypeStruct((N_TOK * ROW_TILE, LANES), jnp.uint32),
            jax.ShapeDtypeStruct((ROUTE_ROWS, N_TOK), jnp.int32),
            jax.ShapeDtypeStruct((N_TOK, LANES), F32),
            jax.ShapeDtypeStruct((1, LANES), F32),
        ],
        scratch_shapes=[
            pltpu.VMEM((ts + 2 * HALO, D_GROUP), F32),
            pltpu.VMEM((ts + 2 * HALO, D_GROUP), F32),
            pltpu.VMEM((1, LANES), F32),
        ],
        compiler_params=_params(("arbitrary", "arbitrary")),
        name="mixer",
    )(x3, f3, z3, z3, z3, z3, z3, z3, z3, z3, z3, z3, z3, z3, wf, ws, bsb, cw, wp, ps, on, wo,
      fg, wr, wrl, br, tri)


def _route(x, g_ref, wr_ref, wrl_ref, br_ref, tri_ref, hp_ref, ri_ref, rg_ref, cnt_ref, carry):
    tr = x.shape[0]
    ms = jnp.mean(x * x, axis=-1, keepdims=True)
    h = (x * lax.rsqrt(ms + EPS)) * g_ref[...]

    _store_row_tiles(hp_ref, _pack_pairs(h))

    h_hi = h.astype(BF16)
    h_lo = (h - h_hi.astype(F32)).astype(BF16)
    logits = (jnp.dot(h_hi, wr_ref[...], preferred_element_type=F32)
              + jnp.dot(h_lo, wr_ref[...], preferred_element_type=F32)
              + jnp.dot(h_hi, wrl_ref[...], preferred_element_type=F32)) + br_ref[...]
    lane = lax.broadcasted_iota(jnp.int32, (tr, LANES), 1)
    neg = np.float32(-np.inf)

    def first_argmax(v, vmax):
        return jnp.min(jnp.where(v == vmax, lane, LANES), axis=-1, keepdims=True)

    mg = jnp.where(lane < N_EXPERT_GROUPS, logits, neg)
    gmax = jnp.max(mg, axis=-1, keepdims=True)
    grp = first_argmax(mg, gmax)
    p_grp = 1.0 / jnp.sum(jnp.exp(mg - gmax), axis=-1, keepdims=True)

    lo = ROUTE_OFF + EXPERTS_PER_GROUP * grp
    le = jnp.where(lane >= lo, jnp.where(lane < lo + EXPERTS_PER_GROUP, logits, neg), neg)
    emax = jnp.max(le, axis=-1, keepdims=True)
    i1 = first_argmax(le, emax)
    le2 = jnp.where(lane == i1, neg, le)
    emax2 = jnp.max(le2, axis=-1, keepdims=True)
    i2 = first_argmax(le2, emax2)
    e2 = jnp.exp(emax2 - emax)
    gate1 = p_grp / (1.0 + e2)
    gate2 = p_grp * e2 / (1.0 + e2)

    is1, is2 = lane == i1, lane == i2
    oh = jnp.where(is1, 1.0, jnp.where(is2, 1.0, 0.0))
    pref = jnp.dot(tri_ref[...], oh.astype(BF16), preferred_element_type=F32) + carry[...]
    r1 = jnp.sum(jnp.where(is1, pref, 0.0), axis=-1, keepdims=True).astype(jnp.int32)
    r2 = jnp.sum(jnp.where(is2, pref, 0.0), axis=-1, keepdims=True).astype(jnp.int32)
    carry[...] = carry[...] + jnp.sum(oh, axis=0, keepdims=True)
    cnt_ref[...] = carry[...]

    ri = jnp.where(lane == 0, i1 - ROUTE_OFF,
                   jnp.where(lane == 1, i2 - ROUTE_OFF,
                             jnp.where(lane == 2, r1, jnp.where(lane == 3, r2, 0))))
    ri_ref[...] = ri.T[0:ri_ref.shape[0], :]
    rg_ref[...] = jnp.where(lane == 0, gate1, jnp.where(lane == 1, gate2, 0.0))


def _invert_body(dest_ref, cnt_ref, rowv_ref, plan_ref):
    def zero_rows(lo, hi):
        def zero(r, c):
            rowv_ref[r] = 0
            return c

        lax.fori_loop(lo, hi, zero, 0)

    def per_expert(e, carry):
        cursor, nonempty = carry
        c = cnt_ref[e]
        nb = (c + (TM_MOE - 1)) // TM_MOE
        zero_rows(cursor * TM_MOE + c, (cursor + nb) * TM_MOE)

        def per_block(j, c2):
            plan_ref[PLAN_EXP + cursor + j] = e
            plan_ref[PLAN_VALID + cursor + j] = jnp.minimum(c - j * TM_MOE, TM_MOE)
            plan_ref[PLAN_SLOT + cursor + j] = nonempty & 1
            return c2

        lax.fori_loop(0, nb, per_block, 0)
        return cursor + nb, nonempty + jnp.where(nb > 0, 1, 0)

    n_active, _ = lax.fori_loop(0, N_EXPERTS, per_expert, (jnp.int32(0), jnp.int32(0)))
    plan_ref[PLAN_ACTIVE] = n_active
    zero_rows(n_active * TM_MOE, N_ROWS)

    def unused_block(b, c):
        plan_ref[PLAN_EXP + b] = plan_ref[PLAN_EXP + n_active - 1]
        plan_ref[PLAN_VALID + b] = 0
        plan_ref[PLAN_SLOT + b] = 0
        plan_ref[PLAN_NEXT + b] = -1
        return c

    lax.fori_loop(n_active, N_BLOCKS, unused_block, 0)

    def next_expert(j, carry):
        cur, nxt = carry
        b = n_active - 1 - j
        e = plan_ref[PLAN_EXP + b]
        nxt = jnp.where(e != cur, cur, nxt)
        plan_ref[PLAN_NEXT + b] = nxt
        return e, nxt

    lax.fori_loop(0, n_active, next_expert, (plan_ref[PLAN_EXP + n_active - 1], jnp.int32(-1)))

    unroll = 16

    def body(j, c):
        rows = [dest_ref[j * unroll + u] for u in range(unroll)]
        for u in range(unroll):
            rowv_ref[rows[u]] = j * unroll + u
        return c

    lax.fori_loop(0, 2 * N_TOK // unroll, body, 0)


def _invert(dest, counts):
    return pl.pallas_call(
        _invert_body,
        grid_spec=pltpu.PrefetchScalarGridSpec(
            num_scalar_prefetch=2,
            grid=(1,),
            in_specs=[],
            out_specs=[pl.BlockSpec(memory_space=pltpu.SMEM), pl.BlockSpec(memory_space=pltpu.SMEM)],
        ),
        out_shape=[jax.ShapeDtypeStruct((N_ROWS,), jnp.int32),
                   jax.ShapeDtypeStruct((PLAN_LEN,), jnp.int32)],
        compiler_params=_params(("arbitrary",)),
        name="invert",
    )(dest, counts)


def _moe_body(plan_ref, rowv_ref, hp_ref, wg_hbm, wu_hbm, wd_hbm, yk_ref, wg32, wu32, wd32,
              wg_ref, wu_ref, wd_ref, xbuf, ybuf, yacc, gsem, ssem, wsem, *, layer):
    i = pl.program_id(0)
    na = plan_ref[PLAN_ACTIVE]

    class _Field:
        def __init__(self, base):
            self.base = base

        def __getitem__(self, b):
            return plan_ref[self.base + b]

    be_ref, nv_ref, par_ref, nxe_ref = (_Field(b) for b in (PLAN_EXP, PLAN_VALID, PLAN_SLOT, PLAN_NEXT))

    def weight_copies(e, p):
        return [pltpu.make_async_copy(src.at[layer, e], dst.at[p], wsem.at[p])
                for src, dst in ((wg_hbm, wg32), (wu_hbm, wu32), (wd_hbm, wd32))]
    slot = lax.rem(i, 2)
    other = 1 - slot
    blk_rows = TM_MOE * ROW_TILE

    def tile_rows(r):
        start = r * ROW_TILE
        return pl.ds(start if isinstance(r, int) else pl.multiple_of(start, ROW_TILE), ROW_TILE)

    def gather_copy(blk, s, r):
        src = (rowv_ref[blk * TM_MOE + r] & (N_TOK - 1)) * ROW_TILE
        return pltpu.make_async_copy(hp_ref.at[pl.ds(pl.multiple_of(src, ROW_TILE), ROW_TILE), :],
                                     xbuf.at[s, tile_rows(r), :], gsem.at[s])

    def scatter_copy(blk, s, r):
        dst = rowv_ref[blk * TM_MOE + r] * ROW_TILE
        return pltpu.make_async_copy(ybuf.at[s, tile_rows(r), :],
                                     yk_ref.at[pl.ds(pl.multiple_of(dst, ROW_TILE), ROW_TILE), :],
                                     ssem.at[s])

    def wait_gather(s):
        pltpu.make_async_copy(hp_ref.at[pl.ds(0, blk_rows), :], xbuf.at[s], gsem.at[s]).wait()

    def wait_scatter(blk, s):
        n = nv_ref[blk]

        @pl.when(n == TM_MOE)
        def _():
            pltpu.make_async_copy(ybuf.at[s], yk_ref.at[pl.ds(0, blk_rows), :], ssem.at[s]).wait()

        @pl.when(n < TM_MOE)
        def _():
            lax.fori_loop(0, n, lambda r, c: (scatter_copy(blk, s, r).wait(), c)[1], 0)

    yslot = lax.rem(i, 3)
    yprev = lax.rem(i + 2, 3)
    nv_prev = nv_ref[jnp.maximum(i - 1, 0)]
    prev_full = (i >= 1) & (nv_prev == TM_MOE)

    def start(copy, priority=ROW_DMA_PRIORITY):
        copy.start(priority=priority)

    @pl.when(i == 0)
    def _():
        for c in weight_copies(be_ref[0], 0):
            c.start()
        lax.fori_loop(0, TM_MOE, lambda r, c: (start(gather_copy(0, 0, r)), c)[1], 0)

    @pl.when((i < na) & ((i == 0) | (be_ref[i] != be_ref[jnp.maximum(i - 1, 0)])))
    def _():
        p = par_ref[i]
        for c in weight_copies(be_ref[i], p):
            c.wait()

        @pl.when(nxe_ref[i] >= 0)
        def _():
            for c in weight_copies(nxe_ref[i], 1 - p):
                c.start()

        wg_ref[...] = wg32[p].astype(BF16)
        wu_ref[...] = wu32[p].astype(BF16)
        wd_ref[...] = wd32[p].astype(BF16)

    @pl.when((i >= 3) & (i < na))
    def _():
        wait_scatter(i - 3, yslot)

    @pl.when((i >= 1) & (i < na) & (nv_prev < TM_MOE))
    def _():
        lax.fori_loop(0, nv_prev, lambda r, c: (start(scatter_copy(i - 1, yprev, r)), c)[1], 0)

    def block_body(scatter_prev):
        wait_gather(slot)
        xu = _load_row_tiles(xbuf.at[slot], TM_MOE)
        lo = pltpu.bitcast(xu << 16, F32).astype(BF16)
        hi = pltpu.bitcast(xu & jnp.uint32(0xFFFF0000), F32).astype(BF16)
        xb = jnp.concatenate([lo, hi], axis=1)
        nxt = jnp.minimum(i + 1, na - 1)
        for r in range(TM_MOE):
            start(gather_copy(nxt, other, r), priority=r % 2)
            if scatter_prev:
                start(scatter_copy(i - 1, yprev, r), priority=(r + 1) % 2)
        fc = 256
        for c in range(D_EXPERT // fc):
            cs = slice(c * fc, (c + 1) * fc)
            g = jnp.dot(xb, wg_ref[:, cs], preferred_element_type=F32)
            u = jnp.dot(xb, wu_ref[:, cs], preferred_element_type=F32)
            a = (g * jax.nn.sigmoid(g) * u).astype(BF16)
            part = jnp.dot(a, wd_ref[cs, :], preferred_element_type=F32)
            if c == 0:
                yacc[...] = part
            else:
                yacc[...] += part
        _store_row_tiles(ybuf.at[yslot], _pack_pairs(yacc[...]))

    @pl.when((i < na) & prev_full)
    def _():
        block_body(True)

    @pl.when((i < na) & jnp.logical_not(prev_full))
    def _():
        block_body(False)

    @pl.when(i == na - 1)
    def _():
        lax.fori_loop(0, nv_ref[i], lambda r, c: (start(scatter_copy(i, yslot, r)), c)[1], 0)
        wait_gather(other)
        wait_scatter(i, yslot)

        @pl.when(i >= 1)
        def _():
            wait_scatter(i - 1, yprev)

        @pl.when(i >= 2)
        def _():
            wait_scatter(i - 2, lax.rem(i + 1, 3))


def _moe(l, plan, rowv, hp, wg, wu, wd):
    return pl.pallas_call(
        functools.partial(_moe_body, layer=l),
        grid_spec=pltpu.PrefetchScalarGridSpec(
            num_scalar_prefetch=2,
            grid=(N_BLOCKS,),
            in_specs=[pl.BlockSpec(memory_space=pl.ANY)] * 4,
            out_specs=pl.BlockSpec(memory_space=pl.ANY),
            scratch_shapes=[
                pltpu.VMEM((2, D_MODEL, D_EXPERT), F32),
                pltpu.VMEM((2, D_MODEL, D_EXPERT), F32),
                pltpu.VMEM((2, D_EXPERT, D_MODEL), F32),
                pltpu.VMEM((D_MODEL, D_EXPERT), BF16),
                pltpu.VMEM((D_MODEL, D_EXPERT), BF16),
                pltpu.VMEM((D_EXPERT, D_MODEL), BF16),
                pltpu.VMEM((2, TM_MOE * ROW_TILE, LANES), jnp.uint32),
                pltpu.VMEM((3, TM_MOE * ROW_TILE, LANES), jnp.uint32),
                pltpu.VMEM((TM_MOE, D_MODEL), F32),
                pltpu.SemaphoreType.DMA((2,)),
                pltpu.SemaphoreType.DMA((3,)),
                pltpu.SemaphoreType.DMA((2,)),
            ],
        ),
        out_shape=jax.ShapeDtypeStruct((2 * N_TOK * ROW_TILE, LANES), jnp.uint32),
        compiler_params=_params(("arbitrary",), vmem=60 * 1024 * 1024),
        name="moe",
    )(plan, rowv, hp, wg, wu, wd)


def _final_body(x_ref, y0_ref, y1_ref, rg_ref, g_ref, o_ref):
    y = _moe_residual(x_ref, y0_ref, y1_ref, rg_ref)
    ms = jnp.mean(y * y, axis=-1, keepdims=True)
    o_ref[...] = (y * lax.rsqrt(ms + EPS)) * g_ref[...]


def _final(x, yk, rg, g):
    tm = 512
    row = pl.BlockSpec((tm, D_MODEL), lambda i: (i, 0))
    return pl.pallas_call(
        _final_body,
        grid=(N_TOK // tm,),
        in_specs=[
            row,
            pl.BlockSpec((None, tm * ROW_TILE, LANES), lambda i: (0, i, 0)),
            pl.BlockSpec((None, tm * ROW_TILE, LANES), lambda i: (1, i, 0)),
            pl.BlockSpec((tm, LANES), lambda i: (i, 0)),
            pl.BlockSpec((1, D_MODEL), lambda i: (0, 0)),
        ],
        out_specs=row,
        out_shape=jax.ShapeDtypeStruct((N_TOK, D_MODEL), F32),
        compiler_params=_params(("parallel",)),
        name="final",
    )(x, yk, yk, rg, g)


def _plan(ri, cnt):
    counts = cnt[0, ROUTE_OFF:ROUTE_OFF + N_EXPERTS].astype(jnp.int32)
    nblk = (counts + TM_MOE - 1) // TM_MOE
    pad_start = (jnp.cumsum(nblk) - nblk) * TM_MOE
    dest = ri[2:4]
    for e in range(N_EXPERTS):
        dest = dest + jnp.where(ri[0:2] == e, pad_start[e], 0)
    rowv, plan = _invert(dest.reshape(-1).astype(jnp.int32), counts)
    return plan, rowv


def kernel(x, mix_norm, w_in, w_fourier, w_spatial, b_spatial, conv_w, w_pool, pool_scale,
           out_norm, w_out, ffn_norm, w_group, b_group, w_router, b_router, w_gate, w_up,
           w_down, final_norm):
    cs, m2 = _fourier_consts()
    tr = 512
    tri = (jnp.arange(tr)[:, None] > jnp.arange(tr)[None, :]).astype(BF16)

    w_in_b = w_in.astype(BF16)
    w_f_b = w_fourier.astype(BF16)
    w_s_b = w_spatial.astype(BF16)
    w_p_b = w_pool.astype(BF16)
    w_o_b = w_out.astype(BF16)
    pad = LANES - N_EXPERT_GROUPS - N_EXPERTS
    wr = jnp.concatenate([w_group, w_router, jnp.zeros((DEPTH, D_MODEL, pad), F32)], axis=-1)
    wr_hi = wr.astype(BF16)
    wr_lo = (wr - wr_hi.astype(F32)).astype(BF16)
    br = jnp.concatenate([b_group, b_router, jnp.zeros((DEPTH, pad), F32)], axis=-1)[:, None, :]
    bsb = jnp.broadcast_to(b_spatial[:, :, :, None], (DEPTH, N_HEADS, CHUNK, HEAD_DIM))
    mix_g = mix_norm[:, None, :]
    ffn_g = ffn_norm[:, None, :]
    out_g = out_norm.reshape(DEPTH, 1, D_MODEL)
    pool_s = pool_scale[:, None, :]

    xt = x.reshape(N_TOK, D_MODEL)
    moe = None
    for l in range(DEPTH):
        if moe is None:
            z = _inproj(l, xt, mix_g, w_in_b)
        else:
            z, xt = _inproj(l, xt, mix_g, w_in_b, moe=moe)
        z3 = z.reshape(BATCH, SEQ, D_IN_PROJ)
        f3 = _fourier(z3, cs, m2)
        x3, hp, ri, rg, cnt = _mixer(l, xt.reshape(BATCH, SEQ, D_MODEL), f3, z3, w_f_b, w_s_b, bsb,
                                     conv_w, w_p_b, pool_s, out_g, w_o_b, ffn_g, wr_hi, wr_lo, br, tri)
        xt = x3.reshape(N_TOK, D_MODEL)
        plan = _plan(ri, cnt)
        yk = _moe(l, *plan, hp, w_gate, w_up, w_down)
        moe = (yk.reshape(2, N_TOK * ROW_TILE, LANES), rg)
    xt = _final(xt, moe[0], moe[1], final_norm[None, :])
    return xt.reshape(BATCH, SEQ, D_MODEL)
```

```python
import functools

import numpy as np
import jax
import jax.numpy as jnp
from jax import lax
from jax.experimental import pallas as pl
from jax.experimental.pallas import tpu as pltpu

D_MODEL = 2048
BATCH = 4
SEQ = 4096
DEPTH = 4
N_TOK = BATCH * SEQ
D_GROUP = 512
HEAD_DIM = 128
N_HEADS = 4
CHUNK = 128
POOL_WINDOWS = (2, 4, 8, 16)
D_IN_PROJ = 7 * D_GROUP
N_EXPERT_GROUPS = 4
EXPERTS_PER_GROUP = 8
N_EXPERTS = 32
D_EXPERT = 768
EPS = 1e-6

F32 = jnp.float32
BF16 = jnp.bfloat16

FFT_Q = 8
FFT_P = SEQ // FFT_Q
FFT_HEADS = 2
FFT_W = FFT_HEADS * HEAD_DIM
HALO = 16
LANES = 128
ROUTE_OFF = N_EXPERT_GROUPS
ROUTE_ROWS = 4
MIX_ROWS = 256

TM_MOE = 256
N_BLOCKS = (2 * N_TOK) // TM_MOE + N_EXPERTS
N_ROWS = N_BLOCKS * TM_MOE
PLAN_EXP, PLAN_VALID, PLAN_SLOT, PLAN_NEXT = (k * N_BLOCKS for k in range(4))
PLAN_ACTIVE = 4 * N_BLOCKS
PLAN_LEN = PLAN_ACTIVE + 1
D_HALF = D_MODEL // 2
ROW_TILE = D_HALF // LANES
ROW_DMA_PRIORITY = 1

VMEM_LIMIT = 56 * 1024 * 1024


def _params(sem, vmem=VMEM_LIMIT):
    return pltpu.CompilerParams(dimension_semantics=sem, vmem_limit_bytes=vmem)


def _unpack_pairs(u):
    lo = pltpu.bitcast(u << 16, F32)
    hi = pltpu.bitcast(u & jnp.uint32(0xFFFF0000), F32)
    return jnp.concatenate([lo, hi], axis=1)


def _pack_pairs(v):
    bits = pltpu.bitcast(v.astype(BF16).astype(F32), jnp.uint32)
    return (bits[:, :D_HALF] >> 16) | (bits[:, D_HALF:] & jnp.uint32(0xFFFF0000))


def _store_row_tiles(ref, packed):
    rows = packed.shape[0]
    for s in range(ROW_TILE):
        ref[pl.ds(s, rows, stride=ROW_TILE), :] = packed[:, s * LANES:(s + 1) * LANES]


def _load_row_tiles(ref, rows):
    return jnp.concatenate(
        [ref[pl.ds(s, rows, stride=ROW_TILE), :] for s in range(ROW_TILE)], axis=1)


def _moe_residual(x_ref, y0_ref, y1_ref, rg_ref):
    rg = rg_ref[...]
    rows = x_ref.shape[0]
    return (x_ref[...] + rg[:, 0:1] * _unpack_pairs(_load_row_tiles(y0_ref, rows))
            + rg[:, 1:2] * _unpack_pairs(_load_row_tiles(y1_ref, rows)))


def _project(x, g_ref, w_ref, z_ref):
    ms = jnp.mean(x * x, axis=-1, keepdims=True)
    h = ((x * lax.rsqrt(ms + EPS)) * g_ref[...]).astype(BF16)
    tn = 512
    for j in range(D_IN_PROJ // tn):
        cs = slice(j * tn, (j + 1) * tn)
        z_ref[:, cs] = jnp.dot(h, w_ref[:, cs], preferred_element_type=F32).astype(BF16)


def _inproj_body(x_ref, g_ref, w_ref, z_ref):
    _project(x_ref[...], g_ref, w_ref, z_ref)


def _inproj_moe_body(x_ref, y0_ref, y1_ref, rg_ref, g_ref, w_ref, z_ref, xo_ref):
    x = _moe_residual(x_ref, y0_ref, y1_ref, rg_ref)
    xo_ref[...] = x
    _project(x, g_ref, w_ref, z_ref)


def _inproj(l, x, g, w, moe=None):
    tm = 512
    row = pl.BlockSpec((tm, D_MODEL), lambda i: (i, 0))
    w_specs = [
        pl.BlockSpec((None, 1, D_MODEL), lambda i: (l, 0, 0)),
        pl.BlockSpec((None, D_MODEL, D_IN_PROJ), lambda i: (l, 0, 0), pipeline_mode=pl.Buffered(1)),
    ]
    z_spec = pl.BlockSpec((tm, D_IN_PROJ), lambda i: (i, 0))
    z_shape = jax.ShapeDtypeStruct((N_TOK, D_IN_PROJ), BF16)
    if moe is None:
        return pl.pallas_call(
            _inproj_body, grid=(N_TOK // tm,), in_specs=[row] + w_specs, out_specs=z_spec,
            out_shape=z_shape, compiler_params=_params(("parallel",)), name="inproj",
        )(x, g, w)
    yk, rg = moe
    return pl.pallas_call(
        _inproj_moe_body,
        grid=(N_TOK // tm,),
        in_specs=[
            row,
            pl.BlockSpec((None, tm * ROW_TILE, LANES), lambda i: (0, i, 0)),
            pl.BlockSpec((None, tm * ROW_TILE, LANES), lambda i: (1, i, 0)),
            pl.BlockSpec((tm, LANES), lambda i: (i, 0)),
        ] + w_specs,
        out_specs=[z_spec, row],
        out_shape=[z_shape, jax.ShapeDtypeStruct((N_TOK, D_MODEL), F32)],
        compiler_params=_params(("parallel",)),
        name="inproj_moe",
    )(x, yk, yk, rg, g, w)


def _fourier_consts():
    c = np.arange(HEAD_DIM)
    ang = 2.0 * np.pi * (np.outer(c, c) % HEAD_DIM) / HEAD_DIM
    cc, sc = np.cos(ang), np.sin(ang)
    scale = 1.0 / np.sqrt(SEQ * HEAD_DIM)
    cs = np.block([[cc, -sc], [sc, cc]]) * scale
    s1 = np.arange(FFT_P)
    m2 = np.zeros((FFT_Q, FFT_P, 2 * FFT_P), np.float64)
    for k2 in range(FFT_Q):
        k = FFT_Q * np.arange(FFT_P) + k2
        th = 2.0 * np.pi * (np.outer(k, s1) % SEQ) / SEQ
        m2[k2, :, :FFT_P] = np.cos(th)
        m2[k2, :, FFT_P:] = np.sin(th)
    return jnp.asarray(cs, BF16), jnp.asarray(m2, BF16)


def _fourier_body(z_ref, cs_ref, m2_ref, f_ref, y_scr, u_scr):
    rc = 64
    r = np.float32(np.sqrt(0.5))

    def chunk(ci, carry):
        r0 = pl.multiple_of(ci * rc, rc)
        zb = [z_ref[pl.ds(s2 * FFT_P + r0, rc), :].astype(F32) for s2 in range(FFT_Q)]
        e0, e1 = zb[0] + zb[4], zb[0] - zb[4]
        e2, e3 = zb[2] + zb[6], zb[2] - zb[6]
        o0, o1 = zb[1] + zb[5], zb[1] - zb[5]
        o2, o3 = zb[3] + zb[7], zb[3] - zb[7]
        p, q = (o1 - o3) * r, (o1 + o3) * r
        ee, oo = e0 + e2, o0 + o2
        ed, od = e0 - e2, o0 - o2
        zero = jnp.zeros_like(e0)
        ys = [(ee + oo, zero), (e1 + p, -e3 - q), (ed, -od), (e1 - p, e3 - q),
              (ee - oo, zero), (e1 - p, q - e3), (ed, od), (e1 + p, e3 + q)]
        for k2 in range(FFT_Q):
            re, im = ys[k2]
            rows = pl.ds(k2 * FFT_P + r0, rc)
            for h in range(FFT_HEADS):
                hs = slice(h * HEAD_DIM, (h + 1) * HEAD_DIM)
                y_scr[rows, 2 * h * HEAD_DIM:(2 * h + 1) * HEAD_DIM] = re[:, hs].astype(BF16)
                y_scr[rows, (2 * h + 1) * HEAD_DIM:(2 * h + 2) * HEAD_DIM] = im[:, hs].astype(BF16)
        return carry

    lax.fori_loop(0, FFT_P // rc, chunk, 0)

    for k2 in range(FFT_Q):
        for h in range(FFT_HEADS):
            yk = y_scr[k2 * FFT_P:(k2 + 1) * FFT_P, 2 * h * HEAD_DIM:(2 * h + 2) * HEAD_DIM]
            ab = jnp.dot(yk, cs_ref[...], preferred_element_type=F32)
            hs = slice(h * HEAD_DIM, (h + 1) * HEAD_DIM)
            u_scr[k2, 0:FFT_P, hs] = ab[:, :HEAD_DIM].astype(BF16)
            u_scr[k2, FFT_P:2 * FFT_P, hs] = ab[:, HEAD_DIM:].astype(BF16)

    for k2 in range(FFT_Q):
        res = jnp.dot(m2_ref[k2], u_scr[k2], preferred_element_type=F32)
        for h in range(FFT_HEADS):
            f_ref[h, pl.ds(k2, FFT_P, stride=FFT_Q), :] = res[:, h * HEAD_DIM:(h + 1) * HEAD_DIM]


def _fourier(z3, cs, m2):
    return pl.pallas_call(
        _fourier_body,
        grid=(BATCH, N_HEADS // FFT_HEADS),
        in_specs=[
            pl.BlockSpec((None, SEQ, FFT_W), lambda b, h: (b, 0, h)),
            pl.BlockSpec((2 * HEAD_DIM, 2 * HEAD_DIM), lambda b, h: (0, 0)),
            pl.BlockSpec((FFT_Q, FFT_P, 2 * FFT_P), lambda b, h: (0, 0, 0)),
        ],
        out_specs=pl.BlockSpec((None, FFT_HEADS, SEQ, HEAD_DIM), lambda b, h: (b, h, 0, 0)),
        out_shape=jax.ShapeDtypeStruct((BATCH, N_HEADS, SEQ, HEAD_DIM), F32),
        scratch_shapes=[
            pltpu.VMEM((SEQ, 2 * FFT_W), BF16),
            pltpu.VMEM((FFT_Q, 2 * FFT_P, FFT_W), BF16),
        ],
        compiler_params=_params(("parallel", "parallel")),
        name="fourier",
    )(z3, cs, m2)


def _gelu(x):
    return 0.5 * x * (1.0 + lax.erf(x * np.float32(np.sqrt(0.5))))


def _mixer_body(x_ref, f_ref, zu_ref, zv_ref, zb_ref, zc_ref, zval_ref, zp_ref,
                zc_prev, zval_prev, zp_prev, zc_next, zval_next, zp_next,
                wf_ref, ws_ref, bs_ref, cw_ref, wp_ref, ps_ref, on_ref, wo_ref,
                fg_ref, wr_ref, wrl_ref, br_ref, tri_ref,
                o_ref, hp_ref, ri_ref, rg_ref, cnt_ref, ext_a, ext_p, ybf, carry):
    i = pl.program_id(1)
    ts = x_ref.shape[0]
    keep_prev = (i > 0).astype(F32)
    keep_next = (i < pl.num_programs(1) - 1).astype(F32)

    def f32(ref):
        return ref[...].astype(F32)

    ext_a[0:HALO, :] = f32(zc_prev) * f32(zval_prev) * keep_prev
    ext_a[HALO:HALO + ts, :] = f32(zc_ref) * f32(zval_ref)
    ext_a[HALO + ts:2 * HALO + ts, :] = f32(zc_next) * f32(zval_next) * keep_next
    ext_p[0:HALO, :] = f32(zp_prev) * keep_prev
    ext_p[HALO:HALO + ts, :] = f32(zp_ref)
    ext_p[HALO + ts:2 * HALO + ts, :] = f32(zp_next) * keep_next

    def mix_rows(r0, nr):
        rs = slice(r0, r0 + nr)

        def norm_project(y, g):
            gs = slice(g * D_GROUP, (g + 1) * D_GROUP)
            ms = jnp.mean(y * y, axis=-1, keepdims=True)
            ybf[rs, gs] = ((y * lax.rsqrt(ms + EPS)) * on_ref[:, gs]).astype(BF16)

        spec = jnp.concatenate([f_ref[h, rs, :] for h in range(N_HEADS)], axis=1).astype(BF16)
        norm_project(jnp.dot(spec, wf_ref[...], preferred_element_type=F32), 0)

        gu = _gelu(zu_ref[rs, :].astype(F32))
        gv = _gelu(zv_ref[rs, :].astype(F32)).astype(BF16)
        cols = []
        for h in range(N_HEADS):
            hs = slice(h * HEAD_DIM, (h + 1) * HEAD_DIM)
            rows = []
            for n in range(nr // CHUNK):
                vv = gv[n * CHUNK:(n + 1) * CHUNK, hs]
                rows.append(jnp.dot(ws_ref[h], vv, preferred_element_type=F32) + bs_ref[h])
            cols.append(jnp.concatenate(rows, axis=0))
        norm_project(gu * jnp.concatenate(cols, axis=1), 1)

        conv = (cw_ref[0:1, :] * ext_a[HALO - 1 + r0:HALO - 1 + r0 + nr, :]
                + cw_ref[1:2, :] * ext_a[HALO + r0:HALO + r0 + nr, :]
                + cw_ref[2:3, :] * ext_a[HALO + 1 + r0:HALO + 1 + r0 + nr, :])
        norm_project(zb_ref[rs, :].astype(F32) * conv, 2)

        t = i * ts + r0 + lax.broadcasted_iota(jnp.int32, (nr, HEAD_DIM), 0)
        outs = []
        for g, w in enumerate(POOL_WINDOWS):
            gs = slice(g * HEAD_DIM, (g + 1) * HEAD_DIM)
            acc = ext_p[HALO - w // 2 + r0:HALO - w // 2 + r0 + nr, gs]
            for d in range(-w // 2 + 1, w // 2):
                acc = acc + ext_p[HALO + d + r0:HALO + d + r0 + nr, gs]
            cnt = (jnp.minimum(t + w // 2, SEQ) - jnp.maximum(t - w // 2, 0)).astype(F32)
            pg = acc / cnt - ext_p[HALO + r0:HALO + r0 + nr, gs]
            outs.append(jnp.dot(pg.astype(BF16), wp_ref[g], preferred_element_type=F32))
        norm_project(jnp.concatenate(outs, axis=1) * ps_ref[...], 3)

        o_ref[rs, :] = x_ref[rs, :] + jnp.dot(ybf[rs, :], wo_ref[...], preferred_element_type=F32)

    for r0 in range(0, ts, MIX_ROWS):
        mix_rows(r0, MIX_ROWS)

    @pl.when((pl.program_id(0) == 0) & (i == 0))
    def _():
        carry[...] = jnp.zeros_like(carry)

    _route(o_ref[...], fg_ref, wr_ref, wrl_ref, br_ref, tri_ref, hp_ref, ri_ref, rg_ref, cnt_ref,
           carry)


def _mixer(l, x3, f3, z3, wf, ws, bsb, cw, wp, ps, on, wo, fg, wr, wrl, br, tri):
    ts = tri.shape[0]
    nb8 = ts // HALO
    last8 = SEQ // HALO - 1
    n_i = SEQ // ts

    def tok(rows, width):
        return pl.BlockSpec((rows, width), lambda b, i: (b * n_i + i, 0))

    def col(j):
        return pl.BlockSpec((None, ts, D_GROUP), lambda b, i, j=j: (b, i, j))

    def prev(j):
        return pl.BlockSpec((None, HALO, D_GROUP),
                            lambda b, i, j=j: (b, jnp.maximum(i * nb8 - 1, 0), j))

    def nxt(j):
        return pl.BlockSpec((None, HALO, D_GROUP),
                            lambda b, i, j=j: (b, jnp.minimum((i + 1) * nb8, last8), j))

    def full(shape, **kw):
        return pl.BlockSpec((None,) + shape, lambda b, i: (l,) + (0,) * len(shape), **kw)

    return pl.pallas_call(
        _mixer_body,
        grid=(BATCH, SEQ // ts),
        in_specs=[
            pl.BlockSpec((None, ts, D_MODEL), lambda b, i: (b, i, 0)),
            pl.BlockSpec((None, N_HEADS, ts, HEAD_DIM), lambda b, i: (b, 0, i, 0)),
            col(1), col(2), col(3), col(4), col(5), col(6),
            prev(4), prev(5), prev(6), nxt(4), nxt(5), nxt(6),
            full((D_GROUP, D_GROUP)), full((N_HEADS, CHUNK, CHUNK)),
            full((N_HEADS, CHUNK, HEAD_DIM)), full((3, D_GROUP)),
            full((4, HEAD_DIM, HEAD_DIM)), full((1, D_GROUP)), full((1, D_MODEL)),
            full((D_MODEL, D_MODEL), pipeline_mode=pl.Buffered(1)),
            full((1, D_MODEL)), full((D_MODEL, LANES)), full((D_MODEL, LANES)), full((1, LANES)),
            pl.BlockSpec((ts, ts), lambda b, i: (0, 0)),
        ],
        out_specs=[
            pl.BlockSpec((None, ts, D_MODEL), lambda b, i: (b, i, 0)),
            tok(ts * ROW_TILE, LANES),
            pl.BlockSpec((ROUTE_ROWS, ts), lambda b, i: (0, b * n_i + i)),
            tok(ts, LANES),
            pl.BlockSpec((1, LANES), lambda b, i: (0, 0)),
        ],
        out_shape=[
            jax.ShapeDtypeStruct((BATCH, SEQ, D_MODEL), F32),
            jax.ShapeDtypeStruct((N_TOK * ROW_TILE, LANES), jnp.uint32),
            jax.ShapeDtypeStruct((ROUTE_ROWS, N_TOK), jnp.int32),
            jax.ShapeDtypeStruct((N_TOK, LANES), F32),
            jax.ShapeDtypeStruct((1, LANES), F32),
        ],
        scratch_shapes=[
            pltpu.VMEM((ts + 2 * HALO, D_GROUP), F32),
            pltpu.VMEM((ts + 2 * HALO, D_GROUP), F32),
            pltpu.VMEM((ts, D_MODEL), BF16),
            pltpu.VMEM((1, LANES), F32),
        ],
        compiler_params=_params(("arbitrary", "arbitrary")),
        name="mixer",
    )(x3, f3, z3, z3, z3, z3, z3, z3, z3, z3, z3, z3, z3, z3, wf, ws, bsb, cw, wp, ps, on, wo,
      fg, wr, wrl, br, tri)


def _route(x, g_ref, wr_ref, wrl_ref, br_ref, tri_ref, hp_ref, ri_ref, rg_ref, cnt_ref, carry):
    tr = x.shape[0]
    ms = jnp.mean(x * x, axis=-1, keepdims=True)
    h = (x * lax.rsqrt(ms + EPS)) * g_ref[...]

    _store_row_tiles(hp_ref, _pack_pairs(h))

    h_hi = h.astype(BF16)
    h_lo = (h - h_hi.astype(F32)).astype(BF16)
    logits = (jnp.dot(h_hi, wr_ref[...], preferred_element_type=F32)
              + jnp.dot(h_lo, wr_ref[...], preferred_element_type=F32)
              + jnp.dot(h_hi, wrl_ref[...], preferred_element_type=F32)) + br_ref[...]
    lane = lax.broadcasted_iota(jnp.int32, (tr, LANES), 1)
    neg = np.float32(-np.inf)

    def first_argmax(v, vmax):
        return jnp.min(jnp.where(v == vmax, lane, LANES), axis=-1, keepdims=True)

    mg = jnp.where(lane < N_EXPERT_GROUPS, logits, neg)
    gmax = jnp.max(mg, axis=-1, keepdims=True)
    grp = first_argmax(mg, gmax)
    p_grp = 1.0 / jnp.sum(jnp.exp(mg - gmax), axis=-1, keepdims=True)

    lo = ROUTE_OFF + EXPERTS_PER_GROUP * grp
    le = jnp.where(lane >= lo, jnp.where(lane < lo + EXPERTS_PER_GROUP, logits, neg), neg)
    emax = jnp.max(le, axis=-1, keepdims=True)
    i1 = first_argmax(le, emax)
    le2 = jnp.where(lane == i1, neg, le)
    emax2 = jnp.max(le2, axis=-1, keepdims=True)
    i2 = first_argmax(le2, emax2)
    e2 = jnp.exp(emax2 - emax)
    gate1 = p_grp / (1.0 + e2)
    gate2 = p_grp * e2 / (1.0 + e2)

    is1, is2 = lane == i1, lane == i2
    oh = jnp.where(is1, 1.0, jnp.where(is2, 1.0, 0.0))
    pref = jnp.dot(tri_ref[...], oh.astype(BF16), preferred_element_type=F32) + carry[...]
    r1 = jnp.sum(jnp.where(is1, pref, 0.0), axis=-1, keepdims=True).astype(jnp.int32)
    r2 = jnp.sum(jnp.where(is2, pref, 0.0), axis=-1, keepdims=True).astype(jnp.int32)
    carry[...] = carry[...] + jnp.sum(oh, axis=0, keepdims=True)
    cnt_ref[...] = carry[...]

    ri = jnp.where(lane == 0, i1 - ROUTE_OFF,
                   jnp.where(lane == 1, i2 - ROUTE_OFF,
                             jnp.where(lane == 2, r1, jnp.where(lane == 3, r2, 0))))
    ri_ref[...] = ri.T[0:ri_ref.shape[0], :]
    rg_ref[...] = jnp.where(lane == 0, gate1, jnp.where(lane == 1, gate2, 0.0))


def _invert_body(dest_ref, cnt_ref, rowv_ref, plan_ref):
    def zero_rows(lo, hi):
        def zero(r, c):
            rowv_ref[r] = 0
            return c

        lax.fori_loop(lo, hi, zero, 0)

    def per_expert(e, carry):
        cursor, nonempty = carry
        c = cnt_ref[e]
        nb = (c + (TM_MOE - 1)) // TM_MOE
        zero_rows(cursor * TM_MOE + c, (cursor + nb) * TM_MOE)

        def per_block(j, c2):
            plan_ref[PLAN_EXP + cursor + j] = e
            plan_ref[PLAN_VALID + cursor + j] = jnp.minimum(c - j * TM_MOE, TM_MOE)
            plan_ref[PLAN_SLOT + cursor + j] = nonempty & 1
            return c2

        lax.fori_loop(0, nb, per_block, 0)
        return cursor + nb, nonempty + jnp.where(nb > 0, 1, 0)

    n_active, _ = lax.fori_loop(0, N_EXPERTS, per_expert, (jnp.int32(0), jnp.int32(0)))
    plan_ref[PLAN_ACTIVE] = n_active
    zero_rows(n_active * TM_MOE, N_ROWS)

    def unused_block(b, c):
        plan_ref[PLAN_EXP + b] = plan_ref[PLAN_EXP + n_active - 1]
        plan_ref[PLAN_VALID + b] = 0
        plan_ref[PLAN_SLOT + b] = 0
        plan_ref[PLAN_NEXT + b] = -1
        return c

    lax.fori_loop(n_active, N_BLOCKS, unused_block, 0)

    def next_expert(j, carry):
        cur, nxt = carry
        b = n_active - 1 - j
        e = plan_ref[PLAN_EXP + b]
        nxt = jnp.where(e != cur, cur, nxt)
        plan_ref[PLAN_NEXT + b] = nxt
        return e, nxt

    lax.fori_loop(0, n_active, next_expert, (plan_ref[PLAN_EXP + n_active - 1], jnp.int32(-1)))

    unroll = 16

    def body(j, c):
        rows = [dest_ref[j * unroll + u] for u in range(unroll)]
        for u in range(unroll):
            rowv_ref[rows[u]] = j * unroll + u
        return c

    lax.fori_loop(0, 2 * N_TOK // unroll, body, 0)


def _invert(dest, counts):
    return pl.pallas_call(
        _invert_body,
        grid_spec=pltpu.PrefetchScalarGridSpec(
            num_scalar_prefetch=2,
            grid=(1,),
            in_specs=[],
            out_specs=[pl.BlockSpec(memory_space=pltpu.SMEM), pl.BlockSpec(memory_space=pltpu.SMEM)],
        ),
        out_shape=[jax.ShapeDtypeStruct((N_ROWS,), jnp.int32),
                   jax.ShapeDtypeStruct((PLAN_LEN,), jnp.int32)],
        compiler_params=_params(("arbitrary",)),
        name="invert",
    )(dest, counts)


def _moe_body(plan_ref, rowv_ref, hp_ref, wg_hbm, wu_hbm, wd_hbm, yk_ref, wg32, wu32, wd32,
              wg_ref, wu_ref, wd_ref, xbuf, ybuf, gsem, ssem, wsem, *, layer):
    i = pl.program_id(0)
    na = plan_ref[PLAN_ACTIVE]

    class _Field:
        def __init__(self, base):
            self.base = base

        def __getitem__(self, b):
            return plan_ref[self.base + b]

    be_ref, nv_ref, par_ref, nxe_ref = (_Field(b) for b in (PLAN_EXP, PLAN_VALID, PLAN_SLOT, PLAN_NEXT))

    def weight_copies(e, p):
        return [pltpu.make_async_copy(src.at[layer, e], dst.at[p], wsem.at[p])
                for src, dst in ((wg_hbm, wg32), (wu_hbm, wu32), (wd_hbm, wd32))]
    slot = lax.rem(i, 2)
    other = 1 - slot
    blk_rows = TM_MOE * ROW_TILE

    def tile_rows(r):
        start = r * ROW_TILE
        return pl.ds(start if isinstance(r, int) else pl.multiple_of(start, ROW_TILE), ROW_TILE)

    def gather_copy(blk, s, r):
        src = (rowv_ref[blk * TM_MOE + r] & (N_TOK - 1)) * ROW_TILE
        return pltpu.make_async_copy(hp_ref.at[pl.ds(pl.multiple_of(src, ROW_TILE), ROW_TILE), :],
                                     xbuf.at[s, tile_rows(r), :], gsem.at[s])

    def scatter_copy(blk, s, r):
        dst = rowv_ref[blk * TM_MOE + r] * ROW_TILE
        return pltpu.make_async_copy(ybuf.at[s, tile_rows(r), :],
                                     yk_ref.at[pl.ds(pl.multiple_of(dst, ROW_TILE), ROW_TILE), :],
                                     ssem.at[s])

    def wait_gather(s):
        pltpu.make_async_copy(hp_ref.at[pl.ds(0, blk_rows), :], xbuf.at[s], gsem.at[s]).wait()

    def wait_scatter(blk, s):
        n = nv_ref[blk]

        @pl.when(n == TM_MOE)
        def _():
            pltpu.make_async_copy(ybuf.at[s], yk_ref.at[pl.ds(0, blk_rows), :], ssem.at[s]).wait()

        @pl.when(n < TM_MOE)
        def _():
            lax.fori_loop(0, n, lambda r, c: (scatter_copy(blk, s, r).wait(), c)[1], 0)

    yslot = lax.rem(i, 3)
    yprev = lax.rem(i + 2, 3)
    nv_prev = nv_ref[jnp.maximum(i - 1, 0)]
    prev_full = (i >= 1) & (nv_prev == TM_MOE)

    def start(copy, priority=ROW_DMA_PRIORITY):
        copy.start(priority=priority)

    @pl.when(i == 0)
    def _():
        for c in weight_copies(be_ref[0], 0):
            c.start()
        lax.fori_loop(0, TM_MOE, lambda r, c: (start(gather_copy(0, 0, r)), c)[1], 0)

    @pl.when((i < na) & ((i == 0) | (be_ref[i] != be_ref[jnp.maximum(i - 1, 0)])))
    def _():
        p = par_ref[i]
        for c in weight_copies(be_ref[i], p):
            c.wait()

        @pl.when(nxe_ref[i] >= 0)
        def _():
            for c in weight_copies(nxe_ref[i], 1 - p):
                c.start()

        wg_ref[...] = wg32[p].astype(BF16)
        wu_ref[...] = wu32[p].astype(BF16)
        wd_ref[...] = wd32[p].astype(BF16)

    @pl.when((i >= 3) & (i < na))
    def _():
        wait_scatter(i - 3, yslot)

    @pl.when((i >= 1) & (i < na) & (nv_prev < TM_MOE))
    def _():
        lax.fori_loop(0, nv_prev, lambda r, c: (start(scatter_copy(i - 1, yprev, r)), c)[1], 0)

    def block_body(scatter_prev):
        wait_gather(slot)
        xu = _load_row_tiles(xbuf.at[slot], TM_MOE)
        lo = pltpu.bitcast(xu << 16, F32).astype(BF16)
        hi = pltpu.bitcast(xu & jnp.uint32(0xFFFF0000), F32).astype(BF16)
        xb = jnp.concatenate([lo, hi], axis=1)
        nxt = jnp.minimum(i + 1, na - 1)
        for r in range(TM_MOE):
            start(gather_copy(nxt, other, r), priority=r % 2)
            if scatter_prev:
                start(scatter_copy(i - 1, yprev, r), priority=(r + 1) % 2)
        g = jnp.dot(xb, wg_ref[...], preferred_element_type=F32)
        u = jnp.dot(xb, wu_ref[...], preferred_element_type=F32)
        act = (g * jax.nn.sigmoid(g) * u).astype(BF16)
        y = jnp.dot(act, wd_ref[...], preferred_element_type=F32)
        _store_row_tiles(ybuf.at[yslot], _pack_pairs(y))

    @pl.when((i < na) & prev_full)
    def _():
        block_body(True)

    @pl.when((i < na) & jnp.logical_not(prev_full))
    def _():
        block_body(False)

    @pl.when(i == na - 1)
    def _():
        lax.fori_loop(0, nv_ref[i], lambda r, c: (start(scatter_copy(i, yslot, r)), c)[1], 0)
        wait_gather(other)
        wait_scatter(i, yslot)

        @pl.when(i >= 1)
        def _():
            wait_scatter(i - 1, yprev)

        @pl.when(i >= 2)
        def _():
            wait_scatter(i - 2, lax.rem(i + 1, 3))


def _moe(l, plan, rowv, hp, wg, wu, wd):
    return pl.pallas_call(
        functools.partial(_moe_body, layer=l),
        grid_spec=pltpu.PrefetchScalarGridSpec(
            num_scalar_prefetch=2,
            grid=(N_BLOCKS,),
            in_specs=[pl.BlockSpec(memory_space=pl.ANY)] * 4,
            out_specs=pl.BlockSpec(memory_space=pl.ANY),
            scratch_shapes=[
                pltpu.VMEM((2, D_MODEL, D_EXPERT), F32),
                pltpu.VMEM((2, D_MODEL, D_EXPERT), F32),
                pltpu.VMEM((2, D_EXPERT, D_MODEL), F32),
                pltpu.VMEM((D_MODEL, D_EXPERT), BF16),
                pltpu.VMEM((D_MODEL, D_EXPERT), BF16),
                pltpu.VMEM((D_EXPERT, D_MODEL), BF16),
                pltpu.VMEM((2, TM_MOE * ROW_TILE, LANES), jnp.uint32),
                pltpu.VMEM((3, TM_MOE * ROW_TILE, LANES), jnp.uint32),
                pltpu.SemaphoreType.DMA((2,)),
                pltpu.SemaphoreType.DMA((3,)),
                pltpu.SemaphoreType.DMA((2,)),
            ],
        ),
        out_shape=jax.ShapeDtypeStruct((2 * N_TOK * ROW_TILE, LANES), jnp.uint32),
        compiler_params=_params(("arbitrary",), vmem=60 * 1024 * 1024),
        name="moe",
    )(plan, rowv, hp, wg, wu, wd)


def _final_body(x_ref, y0_ref, y1_ref, rg_ref, g_ref, o_ref):
    y = _moe_residual(x_ref, y0_ref, y1_ref, rg_ref)
    ms = jnp.mean(y * y, axis=-1, keepdims=True)
    o_ref[...] = (y * lax.rsqrt(ms + EPS)) * g_ref[...]


def _final(x, yk, rg, g):
    tm = 512
    row = pl.BlockSpec((tm, D_MODEL), lambda i: (i, 0))
    return pl.pallas_call(
        _final_body,
        grid=(N_TOK // tm,),
        in_specs=[
            row,
            pl.BlockSpec((None, tm * ROW_TILE, LANES), lambda i: (0, i, 0)),
            pl.BlockSpec((None, tm * ROW_TILE, LANES), lambda i: (1, i, 0)),
            pl.BlockSpec((tm, LANES), lambda i: (i, 0)),
            pl.BlockSpec((1, D_MODEL), lambda i: (0, 0)),
        ],
        out_specs=row,
        out_shape=jax.ShapeDtypeStruct((N_TOK, D_MODEL), F32),
        compiler_params=_params(("parallel",)),
        name="final",
    )(x, yk, yk, rg, g)


def _plan(ri, cnt):
    counts = cnt[0, ROUTE_OFF:ROUTE_OFF + N_EXPERTS].astype(jnp.int32)
    nblk = (counts + TM_MOE - 1) // TM_MOE
    pad_start = (jnp.cumsum(nblk) - nblk) * TM_MOE
    dest = ri[2:4]
    for e in range(N_EXPERTS):
        dest = dest + jnp.where(ri[0:2] == e, pad_start[e], 0)
    rowv, plan = _invert(dest.reshape(-1).astype(jnp.int32), counts)
    return plan, rowv


def kernel(x, mix_norm, w_in, w_fourier, w_spatial, b_spatial, conv_w, w_pool, pool_scale,
           out_norm, w_out, ffn_norm, w_group, b_group, w_router, b_router, w_gate, w_up,
           w_down, final_norm):
    cs, m2 = _fourier_consts()
    tr = 512
    tri = (jnp.arange(tr)[:, None] > jnp.arange(tr)[None, :]).astype(BF16)

    w_in_b = w_in.astype(BF16)
    w_f_b = w_fourier.astype(BF16)
    w_s_b = w_spatial.astype(BF16)
    w_p_b = w_pool.astype(BF16)
    w_o_b = w_out.astype(BF16)
    pad = LANES - N_EXPERT_GROUPS - N_EXPERTS
    wr = jnp.concatenate([w_group, w_router, jnp.zeros((DEPTH, D_MODEL, pad), F32)], axis=-1)
    wr_hi = wr.astype(BF16)
    wr_lo = (wr - wr_hi.astype(F32)).astype(BF16)
    br = jnp.concatenate([b_group, b_router, jnp.zeros((DEPTH, pad), F32)], axis=-1)[:, None, :]
    bsb = jnp.broadcast_to(b_spatial[:, :, :, None], (DEPTH, N_HEADS, CHUNK, HEAD_DIM))
    mix_g = mix_norm[:, None, :]
    ffn_g = ffn_norm[:, None, :]
    out_g = out_norm.reshape(DEPTH, 1, D_MODEL)
    pool_s = pool_scale[:, None, :]

    xt = x.reshape(N_TOK, D_MODEL)
    moe = None
    for l in range(DEPTH):
        if moe is None:
            z = _inproj(l, xt, mix_g, w_in_b)
        else:
            z, xt = _inproj(l, xt, mix_g, w_in_b, moe=moe)
        z3 = z.reshape(BATCH, SEQ, D_IN_PROJ)
        f3 = _fourier(z3, cs, m2)
        x3, hp, ri, rg, cnt = _mixer(l, xt.reshape(BATCH, SEQ, D_MODEL), f3, z3, w_f_b, w_s_b, bsb,
                                     conv_w, w_p_b, pool_s, out_g, w_o_b, ffn_g, wr_hi, wr_lo, br, tri)
        xt = x3.reshape(N_TOK, D_MODEL)
        plan = _plan(ri, cnt)
        yk = _moe(l, *plan, hp, w_gate, w_up, w_down)
        moe = (yk.reshape(2, N_TOK * ROW_TILE, LANES), rg)
    xt = _final(xt, moe[0], moe[1], final_norm[None, :])
    return xt.reshape(BATCH, SEQ, D_MODEL)
```

```python
import functools

import numpy as np
import jax
import jax.numpy as jnp
from jax import lax
from jax.experimental import pallas as pl
from jax.experimental.pallas import tpu as pltpu

D_MODEL = 2048
BATCH = 4
SEQ = 4096
DEPTH = 4
N_TOK = BATCH * SEQ
D_GROUP = 512
HEAD_DIM = 128
N_HEADS = 4
CHUNK = 128
POOL_WINDOWS = (2, 4, 8, 16)
D_IN_PROJ = 7 * D_GROUP
N_EXPERT_GROUPS = 4
EXPERTS_PER_GROUP = 8
N_EXPERTS = 32
D_EXPERT = 768
EPS = 1e-6

F32 = jnp.float32
BF16 = jnp.bfloat16

FFT_Q = 8
FFT_P = SEQ // FFT_Q
FFT_HEADS = 2
FFT_W = FFT_HEADS * HEAD_DIM
HALO = 16
LANES = 128
ROUTE_OFF = N_EXPERT_GROUPS
ROUTE_ROWS = 4
MIX_ROWS = 256

TM_MOE = 256
N_BLOCKS = (2 * N_TOK) // TM_MOE + N_EXPERTS
N_ROWS = N_BLOCKS * TM_MOE
PLAN_EXP, PLAN_VALID, PLAN_SLOT, PLAN_NEXT = (k * N_BLOCKS for k in range(4))
PLAN_ACTIVE = 4 * N_BLOCKS
PLAN_LEN = PLAN_ACTIVE + 1
D_HALF = D_MODEL // 2
ROW_TILE = D_HALF // LANES
ROW_DMA_PRIORITY = 1

VMEM_LIMIT = 56 * 1024 * 1024


def _params(sem, vmem=VMEM_LIMIT):
    return pltpu.CompilerParams(dimension_semantics=sem, vmem_limit_bytes=vmem)


def _unpack_pairs(u):
    lo = pltpu.bitcast(u << 16, F32)
    hi = pltpu.bitcast(u & jnp.uint32(0xFFFF0000), F32)
    return jnp.concatenate([lo, hi], axis=1)


def _pack_pairs(v):
    bits = pltpu.bitcast(v.astype(BF16).astype(F32), jnp.uint32)
    return (bits[:, :D_HALF] >> 16) | (bits[:, D_HALF:] & jnp.uint32(0xFFFF0000))


def _store_row_tiles(ref, packed):
    rows = packed.shape[0]
    for s in range(ROW_TILE):
        ref[pl.ds(s, rows, stride=ROW_TILE), :] = packed[:, s * LANES:(s + 1) * LANES]


def _load_row_tiles(ref, rows):
    return jnp.concatenate(
        [ref[pl.ds(s, rows, stride=ROW_TILE), :] for s in range(ROW_TILE)], axis=1)


def _moe_residual(x_ref, y0_ref, y1_ref, rg_ref):
    rg = rg_ref[...]
    rows = x_ref.shape[0]
    return (x_ref[...] + rg[:, 0:1] * _unpack_pairs(_load_row_tiles(y0_ref, rows))
            + rg[:, 1:2] * _unpack_pairs(_load_row_tiles(y1_ref, rows)))


def _project(x, g_ref, w_ref, z_ref):
    ms = jnp.mean(x * x, axis=-1, keepdims=True)
    h = ((x * lax.rsqrt(ms + EPS)) * g_ref[...]).astype(BF16)
    tn = 512
    for j in range(D_IN_PROJ // tn):
        cs = slice(j * tn, (j + 1) * tn)
        z_ref[:, cs] = jnp.dot(h, w_ref[:, cs], preferred_element_type=F32).astype(BF16)


def _inproj_body(x_ref, g_ref, w_ref, z_ref):
    _project(x_ref[...], g_ref, w_ref, z_ref)


def _inproj_moe_body(x_ref, y0_ref, y1_ref, rg_ref, g_ref, w_ref, z_ref, xo_ref):
    x = _moe_residual(x_ref, y0_ref, y1_ref, rg_ref)
    xo_ref[...] = x
    _project(x, g_ref, w_ref, z_ref)


def _inproj(l, x, g, w, moe=None):
    tm = 512
    row = pl.BlockSpec((tm, D_MODEL), lambda i: (i, 0))
    w_specs = [
        pl.BlockSpec((None, 1, D_MODEL), lambda i: (l, 0, 0)),
        pl.BlockSpec((None, D_MODEL, D_IN_PROJ), lambda i: (l, 0, 0), pipeline_mode=pl.Buffered(1)),
    ]
    z_spec = pl.BlockSpec((tm, D_IN_PROJ), lambda i: (i, 0))
    z_shape = jax.ShapeDtypeStruct((N_TOK, D_IN_PROJ), BF16)
    if moe is None:
        return pl.pallas_call(
            _inproj_body, grid=(N_TOK // tm,), in_specs=[row] + w_specs, out_specs=z_spec,
            out_shape=z_shape, compiler_params=_params(("parallel",)), name="inproj",
        )(x, g, w)
    yk, rg = moe
    return pl.pallas_call(
        _inproj_moe_body,
        grid=(N_TOK // tm,),
        in_specs=[
            row,
            pl.BlockSpec((None, tm * ROW_TILE, LANES), lambda i: (0, i, 0)),
            pl.BlockSpec((None, tm * ROW_TILE, LANES), lambda i: (1, i, 0)),
            pl.BlockSpec((tm, LANES), lambda i: (i, 0)),
        ] + w_specs,
        out_specs=[z_spec, row],
        out_shape=[z_shape, jax.ShapeDtypeStruct((N_TOK, D_MODEL), F32)],
        compiler_params=_params(("parallel",)),
        name="inproj_moe",
    )(x, yk, yk, rg, g, w)


def _fourier_consts():
    c = np.arange(HEAD_DIM)
    ang = 2.0 * np.pi * (np.outer(c, c) % HEAD_DIM) / HEAD_DIM
    cc, sc = np.cos(ang), np.sin(ang)
    scale = 1.0 / np.sqrt(SEQ * HEAD_DIM)
    cs = np.block([[cc, -sc], [sc, cc]]) * scale
    s1 = np.arange(FFT_P)
    m2 = np.zeros((FFT_Q, FFT_P, 2 * FFT_P), np.float64)
    for k2 in range(FFT_Q):
        k = FFT_Q * np.arange(FFT_P) + k2
        th = 2.0 * np.pi * (np.outer(k, s1) % SEQ) / SEQ
        m2[k2, :, :FFT_P] = np.cos(th)
        m2[k2, :, FFT_P:] = np.sin(th)
    return jnp.asarray(cs, BF16), jnp.asarray(m2, BF16)


def _fourier_body(z_ref, cs_ref, m2_ref, f_ref, y_scr, u_scr):
    rc = 64
    r = np.float32(np.sqrt(0.5))

    def chunk(ci, carry):
        r0 = pl.multiple_of(ci * rc, rc)
        zb = [z_ref[pl.ds(s2 * FFT_P + r0, rc), :].astype(F32) for s2 in range(FFT_Q)]
        e0, e1 = zb[0] + zb[4], zb[0] - zb[4]
        e2, e3 = zb[2] + zb[6], zb[2] - zb[6]
        o0, o1 = zb[1] + zb[5], zb[1] - zb[5]
        o2, o3 = zb[3] + zb[7], zb[3] - zb[7]
        p, q = (o1 - o3) * r, (o1 + o3) * r
        ee, oo = e0 + e2, o0 + o2
        ed, od = e0 - e2, o0 - o2
        zero = jnp.zeros_like(e0)
        ys = [(ee + oo, zero), (e1 + p, -e3 - q), (ed, -od), (e1 - p, e3 - q),
              (ee - oo, zero), (e1 - p, q - e3), (ed, od), (e1 + p, e3 + q)]
        for k2 in range(FFT_Q):
            re, im = ys[k2]
            rows = pl.ds(k2 * FFT_P + r0, rc)
            for h in range(FFT_HEADS):
                hs = slice(h * HEAD_DIM, (h + 1) * HEAD_DIM)
                y_scr[rows, 2 * h * HEAD_DIM:(2 * h + 1) * HEAD_DIM] = re[:, hs].astype(BF16)
                y_scr[rows, (2 * h + 1) * HEAD_DIM:(2 * h + 2) * HEAD_DIM] = im[:, hs].astype(BF16)
        return carry

    lax.fori_loop(0, FFT_P // rc, chunk, 0)

    for k2 in range(FFT_Q):
        for h in range(FFT_HEADS):
            yk = y_scr[k2 * FFT_P:(k2 + 1) * FFT_P, 2 * h * HEAD_DIM:(2 * h + 2) * HEAD_DIM]
            ab = jnp.dot(yk, cs_ref[...], preferred_element_type=F32)
            hs = slice(h * HEAD_DIM, (h + 1) * HEAD_DIM)
            u_scr[k2, 0:FFT_P, hs] = ab[:, :HEAD_DIM].astype(BF16)
            u_scr[k2, FFT_P:2 * FFT_P, hs] = ab[:, HEAD_DIM:].astype(BF16)

    for k2 in range(FFT_Q):
        res = jnp.dot(m2_ref[k2], u_scr[k2], preferred_element_type=F32)
        for h in range(FFT_HEADS):
            f_ref[h, pl.ds(k2, FFT_P, stride=FFT_Q), :] = res[:, h * HEAD_DIM:(h + 1) * HEAD_DIM]


def _fourier(z3, cs, m2):
    return pl.pallas_call(
        _fourier_body,
        grid=(BATCH, N_HEADS // FFT_HEADS),
        in_specs=[
            pl.BlockSpec((None, SEQ, FFT_W), lambda b, h: (b, 0, h)),
            pl.BlockSpec((2 * HEAD_DIM, 2 * HEAD_DIM), lambda b, h: (0, 0)),
            pl.BlockSpec((FFT_Q, FFT_P, 2 * FFT_P), lambda b, h: (0, 0, 0)),
        ],
        out_specs=pl.BlockSpec((None, FFT_HEADS, SEQ, HEAD_DIM), lambda b, h: (b, h, 0, 0)),
        out_shape=jax.ShapeDtypeStruct((BATCH, N_HEADS, SEQ, HEAD_DIM), F32),
        scratch_shapes=[
            pltpu.VMEM((SEQ, 2 * FFT_W), BF16),
            pltpu.VMEM((FFT_Q, 2 * FFT_P, FFT_W), BF16),
        ],
        compiler_params=_params(("parallel", "parallel")),
        name="fourier",
    )(z3, cs, m2)


def _gelu(x):
    return 0.5 * x * (1.0 + lax.erf(x * np.float32(np.sqrt(0.5))))


def _mixer_body(x_ref, f_ref, zu_ref, zv_ref, zb_ref, zc_ref, zval_ref, zp_ref,
                zc_prev, zval_prev, zp_prev, zc_next, zval_next, zp_next,
                wf_ref, ws_ref, bs_ref, cw_ref, wp_ref, ps_ref, on_ref, wo_ref,
                fg_ref, wr_ref, wrl_ref, br_ref, tri_ref,
                o_ref, hp_ref, ri_ref, rg_ref, cnt_ref, ext_a, ext_p, ybf, lg_scr, carry):
    s = pl.program_id(0)
    ts = x_ref.shape[0]
    n_i = SEQ // ts
    i = lax.rem(jnp.minimum(s, pl.num_programs(0) - 2), n_i)
    keep_prev = (i > 0).astype(F32)
    keep_next = (i < n_i - 1).astype(F32)

    @pl.when(s == 0)
    def _():
        carry[...] = jnp.zeros_like(carry)
        lg_scr[...] = jnp.zeros_like(lg_scr)

    _select_experts(lg_scr[...], (s > 0).astype(F32), tri_ref, ri_ref, rg_ref, cnt_ref, carry)

    def f32(ref):
        return ref[...].astype(F32)

    ext_a[0:HALO, :] = f32(zc_prev) * f32(zval_prev) * keep_prev
    ext_a[HALO:HALO + ts, :] = f32(zc_ref) * f32(zval_ref)
    ext_a[HALO + ts:2 * HALO + ts, :] = f32(zc_next) * f32(zval_next) * keep_next
    ext_p[0:HALO, :] = f32(zp_prev) * keep_prev
    ext_p[HALO:HALO + ts, :] = f32(zp_ref)
    ext_p[HALO + ts:2 * HALO + ts, :] = f32(zp_next) * keep_next

    def mix_rows(r0, nr):
        rs = slice(r0, r0 + nr)

        def norm_project(y, g):
            gs = slice(g * D_GROUP, (g + 1) * D_GROUP)
            ms = jnp.mean(y * y, axis=-1, keepdims=True)
            ybf[rs, gs] = ((y * lax.rsqrt(ms + EPS)) * on_ref[:, gs]).astype(BF16)

        spec = jnp.concatenate([f_ref[h, rs, :] for h in range(N_HEADS)], axis=1).astype(BF16)
        norm_project(jnp.dot(spec, wf_ref[...], preferred_element_type=F32), 0)

        gu = _gelu(zu_ref[rs, :].astype(F32))
        gv = _gelu(zv_ref[rs, :].astype(F32)).astype(BF16)
        cols = []
        for h in range(N_HEADS):
            hs = slice(h * HEAD_DIM, (h + 1) * HEAD_DIM)
            rows = []
            for n in range(nr // CHUNK):
                vv = gv[n * CHUNK:(n + 1) * CHUNK, hs]
                rows.append(jnp.dot(ws_ref[h], vv, preferred_element_type=F32) + bs_ref[h])
            cols.append(jnp.concatenate(rows, axis=0))
        norm_project(gu * jnp.concatenate(cols, axis=1), 1)

        conv = (cw_ref[0:1, :] * ext_a[HALO - 1 + r0:HALO - 1 + r0 + nr, :]
                + cw_ref[1:2, :] * ext_a[HALO + r0:HALO + r0 + nr, :]
                + cw_ref[2:3, :] * ext_a[HALO + 1 + r0:HALO + 1 + r0 + nr, :])
        norm_project(zb_ref[rs, :].astype(F32) * conv, 2)

        t = i * ts + r0 + lax.broadcasted_iota(jnp.int32, (nr, HEAD_DIM), 0)
        outs = []
        for g, w in enumerate(POOL_WINDOWS):
            gs = slice(g * HEAD_DIM, (g + 1) * HEAD_DIM)
            acc = ext_p[HALO - w // 2 + r0:HALO - w // 2 + r0 + nr, gs]
            for d in range(-w // 2 + 1, w // 2):
                acc = acc + ext_p[HALO + d + r0:HALO + d + r0 + nr, gs]
            cnt = (jnp.minimum(t + w // 2, SEQ) - jnp.maximum(t - w // 2, 0)).astype(F32)
            pg = acc / cnt - ext_p[HALO + r0:HALO + r0 + nr, gs]
            outs.append(jnp.dot(pg.astype(BF16), wp_ref[g], preferred_element_type=F32))
        norm_project(jnp.concatenate(outs, axis=1) * ps_ref[...], 3)

        o_ref[rs, :] = x_ref[rs, :] + jnp.dot(ybf[rs, :], wo_ref[...], preferred_element_type=F32)

    for r0 in range(0, ts, MIX_ROWS):
        mix_rows(r0, MIX_ROWS)

    lg_scr[...] = _route_logits(o_ref[...], fg_ref, wr_ref, wrl_ref, br_ref, hp_ref)


def _mixer(l, x3, f3, z3, wf, ws, bsb, cw, wp, ps, on, wo, fg, wr, wrl, br, tri):
    ts = tri.shape[0]
    nb8 = ts // HALO
    last8 = SEQ // HALO - 1
    n_i = SEQ // ts
    n_tiles = N_TOK // ts

    def cur(s):
        t = jnp.minimum(s, n_tiles - 1)
        return t // n_i, lax.rem(t, n_i)

    def tok(rows, width):
        return pl.BlockSpec((rows, width), lambda s: (jnp.minimum(s, n_tiles - 1), 0))

    def col(j):
        return pl.BlockSpec((None, ts, D_GROUP), lambda s, j=j: cur(s) + (j,))

    def prev(j):
        return pl.BlockSpec((None, HALO, D_GROUP),
                            lambda s, j=j: (cur(s)[0], jnp.maximum(cur(s)[1] * nb8 - 1, 0), j))

    def nxt(j):
        return pl.BlockSpec((None, HALO, D_GROUP),
                            lambda s, j=j: (cur(s)[0], jnp.minimum((cur(s)[1] + 1) * nb8, last8), j))

    def full(shape, **kw):
        return pl.BlockSpec((None,) + shape, lambda s: (l,) + (0,) * len(shape), **kw)

    def done(s):
        return jnp.maximum(s - 1, 0)

    return pl.pallas_call(
        _mixer_body,
        grid=(n_tiles + 1,),
        in_specs=[
            pl.BlockSpec((None, ts, D_MODEL), lambda s: cur(s) + (0,)),
            pl.BlockSpec((None, N_HEADS, ts, HEAD_DIM), lambda s: (cur(s)[0], 0, cur(s)[1], 0)),
            col(1), col(2), col(3), col(4), col(5), col(6),
            prev(4), prev(5), prev(6), nxt(4), nxt(5), nxt(6),
            full((D_GROUP, D_GROUP)), full((N_HEADS, CHUNK, CHUNK)),
            full((N_HEADS, CHUNK, HEAD_DIM)), full((3, D_GROUP)),
            full((4, HEAD_DIM, HEAD_DIM)), full((1, D_GROUP)), full((1, D_MODEL)),
            full((D_MODEL, D_MODEL), pipeline_mode=pl.Buffered(1)),
            full((1, D_MODEL)), full((D_MODEL, LANES)), full((D_MODEL, LANES)), full((1, LANES)),
            pl.BlockSpec((ts, ts), lambda s: (0, 0)),
        ],
        out_specs=[
            pl.BlockSpec((None, ts, D_MODEL), lambda s: cur(s) + (0,)),
            tok(ts * ROW_TILE, LANES),
            pl.BlockSpec((ROUTE_ROWS, ts), lambda s: (0, done(s))),
            pl.BlockSpec((ts, LANES), lambda s: (done(s), 0)),
            pl.BlockSpec((1, LANES), lambda s: (0, 0)),
        ],
        out_shape=[
            jax.ShapeDtypeStruct((BATCH, SEQ, D_MODEL), F32),
            jax.ShapeDtypeStruct((N_TOK * ROW_TILE, LANES), jnp.uint32),
            jax.ShapeDtypeStruct((ROUTE_ROWS, N_TOK), jnp.int32),
            jax.ShapeDtypeStruct((N_TOK, LANES), F32),
            jax.ShapeDtypeStruct((1, LANES), F32),
        ],
        scratch_shapes=[
            pltpu.VMEM((ts + 2 * HALO, D_GROUP), F32),
            pltpu.VMEM((ts + 2 * HALO, D_GROUP), F32),
            pltpu.VMEM((ts, D_MODEL), BF16),
            pltpu.VMEM((ts, LANES), F32),
            pltpu.VMEM((1, LANES), F32),
        ],
        compiler_params=_params(("arbitrary",)),
        name="mixer",
    )(x3, f3, z3, z3, z3, z3, z3, z3, z3, z3, z3, z3, z3, z3, wf, ws, bsb, cw, wp, ps, on, wo,
      fg, wr, wrl, br, tri)


def _route_logits(x, g_ref, wr_ref, wrl_ref, br_ref, hp_ref):
    ms = jnp.mean(x * x, axis=-1, keepdims=True)
    h = (x * lax.rsqrt(ms + EPS)) * g_ref[...]

    _store_row_tiles(hp_ref, _pack_pairs(h))

    h_hi = h.astype(BF16)
    h_lo = (h - h_hi.astype(F32)).astype(BF16)
    return (jnp.dot(h_hi, wr_ref[...], preferred_element_type=F32)
            + jnp.dot(h_lo, wr_ref[...], preferred_element_type=F32)
            + jnp.dot(h_hi, wrl_ref[...], preferred_element_type=F32)) + br_ref[...]


def _select_experts(logits, count_it, tri_ref, ri_ref, rg_ref, cnt_ref, carry):
    tr = logits.shape[0]
    ch = CHUNK
    lane = lax.broadcasted_iota(jnp.int32, (ch, LANES), 1)
    neg = np.float32(-np.inf)

    def first_argmax(v, vmax):
        return jnp.min(jnp.where(v == vmax, lane, LANES), axis=-1, keepdims=True)

    picks = []
    for c in range(tr // ch):
        rows = slice(c * ch, (c + 1) * ch)
        lg = logits[rows, :]
        mg = jnp.where(lane < N_EXPERT_GROUPS, lg, neg)
        gmax = jnp.max(mg, axis=-1, keepdims=True)
        grp = first_argmax(mg, gmax)
        p_grp = 1.0 / jnp.sum(jnp.exp(mg - gmax), axis=-1, keepdims=True)

        lo = ROUTE_OFF + EXPERTS_PER_GROUP * grp
        le = jnp.where(lane >= lo, jnp.where(lane < lo + EXPERTS_PER_GROUP, lg, neg), neg)
        emax = jnp.max(le, axis=-1, keepdims=True)
        i1 = first_argmax(le, emax)
        le2 = jnp.where(lane == i1, neg, le)
        emax2 = jnp.max(le2, axis=-1, keepdims=True)
        i2 = first_argmax(le2, emax2)
        e2 = jnp.exp(emax2 - emax)
        gate1 = p_grp / (1.0 + e2)
        gate2 = p_grp * e2 / (1.0 + e2)
        rg_ref[rows, :] = jnp.where(lane == 0, gate1, jnp.where(lane == 1, gate2, 0.0))
        picks.append((i1, i2))

    oh = jnp.concatenate(
        [jnp.where(lane == i1, 1.0, jnp.where(lane == i2, 1.0, 0.0)) for i1, i2 in picks], axis=0)
    pref = jnp.dot(tri_ref[...], oh.astype(BF16), preferred_element_type=F32) + carry[...]
    carry[...] = carry[...] + count_it * jnp.sum(oh, axis=0, keepdims=True)
    cnt_ref[...] = carry[...]

    for c, (i1, i2) in enumerate(picks):
        rows = slice(c * ch, (c + 1) * ch)
        pc = pref[rows, :]
        r1 = jnp.sum(jnp.where(lane == i1, pc, 0.0), axis=-1, keepdims=True).astype(jnp.int32)
        r2 = jnp.sum(jnp.where(lane == i2, pc, 0.0), axis=-1, keepdims=True).astype(jnp.int32)
        ri = jnp.where(lane == 0, i1 - ROUTE_OFF,
                       jnp.where(lane == 1, i2 - ROUTE_OFF,
                                 jnp.where(lane == 2, r1, jnp.where(lane == 3, r2, 0))))
        ri_ref[:, rows] = ri.T[0:ri_ref.shape[0], :]


def _invert_body(dest_ref, cnt_ref, rowv_ref, plan_ref):
    def zero_rows(lo, hi):
        def zero(r, c):
            rowv_ref[r] = 0
            return c

        lax.fori_loop(lo, hi, zero, 0)

    def per_expert(e, carry):
        cursor, nonempty = carry
        c = cnt_ref[e]
        nb = (c + (TM_MOE - 1)) // TM_MOE
        zero_rows(cursor * TM_MOE + c, (cursor + nb) * TM_MOE)

        def per_block(j, c2):
            plan_ref[PLAN_EXP + cursor + j] = e
            plan_ref[PLAN_VALID + cursor + j] = jnp.minimum(c - j * TM_MOE, TM_MOE)
            plan_ref[PLAN_SLOT + cursor + j] = nonempty & 1
            return c2

        lax.fori_loop(0, nb, per_block, 0)
        return cursor + nb, nonempty + jnp.where(nb > 0, 1, 0)

    n_active, _ = lax.fori_loop(0, N_EXPERTS, per_expert, (jnp.int32(0), jnp.int32(0)))
    plan_ref[PLAN_ACTIVE] = n_active
    zero_rows(n_active * TM_MOE, N_ROWS)

    def unused_block(b, c):
        plan_ref[PLAN_EXP + b] = plan_ref[PLAN_EXP + n_active - 1]
        plan_ref[PLAN_VALID + b] = 0
        plan_ref[PLAN_SLOT + b] = 0
        plan_ref[PLAN_NEXT + b] = -1
        return c

    lax.fori_loop(n_active, N_BLOCKS, unused_block, 0)

    def next_expert(j, carry):
        cur, nxt = carry
        b = n_active - 1 - j
        e = plan_ref[PLAN_EXP + b]
        nxt = jnp.where(e != cur, cur, nxt)
        plan_ref[PLAN_NEXT + b] = nxt
        return e, nxt

    lax.fori_loop(0, n_active, next_expert, (plan_ref[PLAN_EXP + n_active - 1], jnp.int32(-1)))

    unroll = 16

    def body(j, c):
        rows = [dest_ref[j * unroll + u] for u in range(unroll)]
        for u in range(unroll):
            rowv_ref[rows[u]] = j * unroll + u
        return c

    lax.fori_loop(0, 2 * N_TOK // unroll, body, 0)


def _invert(dest, counts):
    return pl.pallas_call(
        _invert_body,
        grid_spec=pltpu.PrefetchScalarGridSpec(
            num_scalar_prefetch=2,
            grid=(1,),
            in_specs=[],
            out_specs=[pl.BlockSpec(memory_space=pltpu.SMEM), pl.BlockSpec(memory_space=pltpu.SMEM)],
        ),
        out_shape=[jax.ShapeDtypeStruct((N_ROWS,), jnp.int32),
                   jax.ShapeDtypeStruct((PLAN_LEN,), jnp.int32)],
        compiler_params=_params(("arbitrary",)),
        name="invert",
    )(dest, counts)


def _moe_body(plan_ref, rowv_ref, hp_ref, wg_hbm, wu_hbm, wd_hbm, yk_ref, wg32, wu32, wd32,
              wg_ref, wu_ref, wd_ref, xbuf, ybuf, gsem, ssem, wsem, *, layer):
    i = pl.program_id(0)
    na = plan_ref[PLAN_ACTIVE]

    class _Field:
        def __init__(self, base):
            self.base = base

        def __getitem__(self, b):
            return plan_ref[self.base + b]

    be_ref, nv_ref, par_ref, nxe_ref = (_Field(b) for b in (PLAN_EXP, PLAN_VALID, PLAN_SLOT, PLAN_NEXT))

    def weight_copies(e, p):
        return [pltpu.make_async_copy(src.at[layer, e], dst.at[p], wsem.at[p])
                for src, dst in ((wg_hbm, wg32), (wu_hbm, wu32), (wd_hbm, wd32))]
    slot = lax.rem(i, 2)
    other = 1 - slot
    blk_rows = TM_MOE * ROW_TILE

    def tile_rows(r):
        start = r * ROW_TILE
        return pl.ds(start if isinstance(r, int) else pl.multiple_of(start, ROW_TILE), ROW_TILE)

    def gather_copy(blk, s, r):
        src = (rowv_ref[blk * TM_MOE + r] & (N_TOK - 1)) * ROW_TILE
        return pltpu.make_async_copy(hp_ref.at[pl.ds(pl.multiple_of(src, ROW_TILE), ROW_TILE), :],
                                     xbuf.at[s, tile_rows(r), :], gsem.at[s])

    def scatter_copy(blk, s, r):
        dst = rowv_ref[blk * TM_MOE + r] * ROW_TILE
        return pltpu.make_async_copy(ybuf.at[s, tile_rows(r), :],
                                     yk_ref.at[pl.ds(pl.multiple_of(dst, ROW_TILE), ROW_TILE), :],
                                     ssem.at[s])

    def wait_gather(s):
        pltpu.make_async_copy(hp_ref.at[pl.ds(0, blk_rows), :], xbuf.at[s], gsem.at[s]).wait()

    def wait_scatter(blk, s):
        n = nv_ref[blk]

        @pl.when(n == TM_MOE)
        def _():
            pltpu.make_async_copy(ybuf.at[s], yk_ref.at[pl.ds(0, blk_rows), :], ssem.at[s]).wait()

        @pl.when(n < TM_MOE)
        def _():
            lax.fori_loop(0, n, lambda r, c: (scatter_copy(blk, s, r).wait(), c)[1], 0)

    yslot = lax.rem(i, 3)
    yprev = lax.rem(i + 2, 3)
    nv_prev = nv_ref[jnp.maximum(i - 1, 0)]
    prev_full = (i >= 1) & (nv_prev == TM_MOE)

    def start(copy, priority=ROW_DMA_PRIORITY):
        copy.start(priority=priority)

    @pl.when(i == 0)
    def _():
        for c in weight_copies(be_ref[0], 0):
            c.start()
        lax.fori_loop(0, TM_MOE, lambda r, c: (start(gather_copy(0, 0, r)), c)[1], 0)

    @pl.when((i < na) & ((i == 0) | (be_ref[i] != be_ref[jnp.maximum(i - 1, 0)])))
    def _():
        p = par_ref[i]
        for c in weight_copies(be_ref[i], p):
            c.wait()

        @pl.when(nxe_ref[i] >= 0)
        def _():
            for c in weight_copies(nxe_ref[i], 1 - p):
                c.start()

        wg_ref[...] = wg32[p].astype(BF16)
        wu_ref[...] = wu32[p].astype(BF16)
        wd_ref[...] = wd32[p].astype(BF16)

    @pl.when((i >= 3) & (i < na))
    def _():
        wait_scatter(i - 3, yslot)

    @pl.when((i >= 1) & (i < na) & (nv_prev < TM_MOE))
    def _():
        lax.fori_loop(0, nv_prev, lambda r, c: (start(scatter_copy(i - 1, yprev, r)), c)[1], 0)

    def block_body(scatter_prev):
        wait_gather(slot)
        xu = _load_row_tiles(xbuf.at[slot], TM_MOE)
        lo = pltpu.bitcast(xu << 16, F32).astype(BF16)
        hi = pltpu.bitcast(xu & jnp.uint32(0xFFFF0000), F32).astype(BF16)
        xb = jnp.concatenate([lo, hi], axis=1)
        nxt = jnp.minimum(i + 1, na - 1)
        for r in range(TM_MOE):
            start(gather_copy(nxt, other, r), priority=r % 2)
            if scatter_prev:
                start(scatter_copy(i - 1, yprev, r), priority=(r + 1) % 2)
        g = jnp.dot(xb, wg_ref[...], preferred_element_type=F32)
        u = jnp.dot(xb, wu_ref[...], preferred_element_type=F32)
        act = (g * jax.nn.sigmoid(g) * u).astype(BF16)
        y = jnp.dot(act, wd_ref[...], preferred_element_type=F32)
        _store_row_tiles(ybuf.at[yslot], _pack_pairs(y))

    @pl.when((i < na) & prev_full)
    def _():
        block_body(True)

    @pl.when((i < na) & jnp.logical_not(prev_full))
    def _():
        block_body(False)

    @pl.when(i == na - 1)
    def _():
        lax.fori_loop(0, nv_ref[i], lambda r, c: (start(scatter_copy(i, yslot, r)), c)[1], 0)
        wait_gather(other)
        wait_scatter(i, yslot)

        @pl.when(i >= 1)
        def _():
            wait_scatter(i - 1, yprev)

        @pl.when(i >= 2)
        def _():
            wait_scatter(i - 2, lax.rem(i + 1, 3))


def _moe(l, plan, rowv, hp, wg, wu, wd):
    return pl.pallas_call(
        functools.partial(_moe_body, layer=l),
        grid_spec=pltpu.PrefetchScalarGridSpec(
            num_scalar_prefetch=2,
            grid=(N_BLOCKS,),
            in_specs=[pl.BlockSpec(memory_space=pl.ANY)] * 4,
            out_specs=pl.BlockSpec(memory_space=pl.ANY),
            scratch_shapes=[
                pltpu.VMEM((2, D_MODEL, D_EXPERT), F32),
                pltpu.VMEM((2, D_MODEL, D_EXPERT), F32),
                pltpu.VMEM((2, D_EXPERT, D_MODEL), F32),
                pltpu.VMEM((D_MODEL, D_EXPERT), BF16),
                pltpu.VMEM((D_MODEL, D_EXPERT), BF16),
                pltpu.VMEM((D_EXPERT, D_MODEL), BF16),
                pltpu.VMEM((2, TM_MOE * ROW_TILE, LANES), jnp.uint32),
                pltpu.VMEM((3, TM_MOE * ROW_TILE, LANES), jnp.uint32),
                pltpu.SemaphoreType.DMA((2,)),
                pltpu.SemaphoreType.DMA((3,)),
                pltpu.SemaphoreType.DMA((2,)),
            ],
        ),
        out_shape=jax.ShapeDtypeStruct((2 * N_TOK * ROW_TILE, LANES), jnp.uint32),
        compiler_params=_params(("arbitrary",), vmem=60 * 1024 * 1024),
        name="moe",
    )(plan, rowv, hp, wg, wu, wd)


def _final_body(x_ref, y0_ref, y1_ref, rg_ref, g_ref, o_ref):
    y = _moe_residual(x_ref, y0_ref, y1_ref, rg_ref)
    ms = jnp.mean(y * y, axis=-1, keepdims=True)
    o_ref[...] = (y * lax.rsqrt(ms + EPS)) * g_ref[...]


def _final(x, yk, rg, g):
    tm = 512
    row = pl.BlockSpec((tm, D_MODEL), lambda i: (i, 0))
    return pl.pallas_call(
        _final_body,
        grid=(N_TOK // tm,),
        in_specs=[
            row,
            pl.BlockSpec((None, tm * ROW_TILE, LANES), lambda i: (0, i, 0)),
            pl.BlockSpec((None, tm * ROW_TILE, LANES), lambda i: (1, i, 0)),
            pl.BlockSpec((tm, LANES), lambda i: (i, 0)),
            pl.BlockSpec((1, D_MODEL), lambda i: (0, 0)),
        ],
        out_specs=row,
        out_shape=jax.ShapeDtypeStruct((N_TOK, D_MODEL), F32),
        compiler_params=_params(("parallel",)),
        name="final",
    )(x, yk, yk, rg, g)


def _plan(ri, cnt):
    counts = cnt[0, ROUTE_OFF:ROUTE_OFF + N_EXPERTS].astype(jnp.int32)
    nblk = (counts + TM_MOE - 1) // TM_MOE
    pad_start = (jnp.cumsum(nblk) - nblk) * TM_MOE
    dest = ri[2:4]
    for e in range(N_EXPERTS):
        dest = dest + jnp.where(ri[0:2] == e, pad_start[e], 0)
    rowv, plan = _invert(dest.reshape(-1).astype(jnp.int32), counts)
    return plan, rowv


def kernel(x, mix_norm, w_in, w_fourier, w_spatial, b_spatial, conv_w, w_pool, pool_scale,
           out_norm, w_out, ffn_norm, w_group, b_group, w_router, b_router, w_gate, w_up,
           w_down, final_norm):
    cs, m2 = _fourier_consts()
    tr = 512
    tri = (jnp.arange(tr)[:, None] > jnp.arange(tr)[None, :]).astype(BF16)

    w_in_b = w_in.astype(BF16)
    w_f_b = w_fourier.astype(BF16)
    w_s_b = w_spatial.astype(BF16)
    w_p_b = w_pool.astype(BF16)
    w_o_b = w_out.astype(BF16)
    pad = LANES - N_EXPERT_GROUPS - N_EXPERTS
    wr = jnp.concatenate([w_group, w_router, jnp.zeros((DEPTH, D_MODEL, pad), F32)], axis=-1)
    wr_hi = wr.astype(BF16)
    wr_lo = (wr - wr_hi.astype(F32)).astype(BF16)
    br = jnp.concatenate([b_group, b_router, jnp.zeros((DEPTH, pad), F32)], axis=-1)[:, None, :]
    bsb = jnp.broadcast_to(b_spatial[:, :, :, None], (DEPTH, N_HEADS, CHUNK, HEAD_DIM))
    mix_g = mix_norm[:, None, :]
    ffn_g = ffn_norm[:, None, :]
    out_g = out_norm.reshape(DEPTH, 1, D_MODEL)
    pool_s = pool_scale[:, None, :]

    xt = x.reshape(N_TOK, D_MODEL)
    moe = None
    for l in range(DEPTH):
        if moe is None:
            z = _inproj(l, xt, mix_g, w_in_b)
        else:
            z, xt = _inproj(l, xt, mix_g, w_in_b, moe=moe)
        z3 = z.reshape(BATCH, SEQ, D_IN_PROJ)
        f3 = _fourier(z3, cs, m2)
        x3, hp, ri, rg, cnt = _mixer(l, xt.reshape(BATCH, SEQ, D_MODEL), f3, z3, w_f_b, w_s_b, bsb,
                                     conv_w, w_p_b, pool_s, out_g, w_o_b, ffn_g, wr_hi, wr_lo, br, tri)
        xt = x3.reshape(N_TOK, D_MODEL)
        plan = _plan(ri, cnt)
        yk = _moe(l, *plan, hp, w_gate, w_up, w_down)
        moe = (yk.reshape(2, N_TOK * ROW_TILE, LANES), rg)
    xt = _final(xt, moe[0], moe[1], final_norm[None, :])
    return xt.reshape(BATCH, SEQ, D_MODEL)
```

```python
import functools

import numpy as np
import jax
import jax.numpy as jnp
from jax import lax
from jax.experimental import pallas as pl
from jax.experimental.pallas import tpu as pltpu

D_MODEL = 2048
BATCH = 4
SEQ = 4096
DEPTH = 4
N_TOK = BATCH * SEQ
D_GROUP = 512
HEAD_DIM = 128
N_HEADS = 4
CHUNK = 128
POOL_WINDOWS = (2, 4, 8, 16)
D_IN_PROJ = 7 * D_GROUP
N_EXPERT_GROUPS = 4
EXPERTS_PER_GROUP = 8
N_EXPERTS = 32
D_EXPERT = 768
EPS = 1e-6

F32 = jnp.float32
BF16 = jnp.bfloat16

FFT_Q = 8
FFT_P = SEQ // FFT_Q
FFT_HEADS = 2
FFT_W = FFT_HEADS * HEAD_DIM
HALO = 16
LANES = 128
ROUTE_OFF = N_EXPERT_GROUPS
ROUTE_ROWS = 4
MIX_ROWS = 256

TM_MOE = 256
N_BLOCKS = (2 * N_TOK) // TM_MOE + N_EXPERTS
N_ROWS = N_BLOCKS * TM_MOE
PLAN_EXP, PLAN_VALID, PLAN_SLOT, PLAN_NEXT = (k * N_BLOCKS for k in range(4))
PLAN_ACTIVE = 4 * N_BLOCKS
PLAN_LEN = PLAN_ACTIVE + 1
D_HALF = D_MODEL // 2
ROW_TILE = D_HALF // LANES
ROW_DMA_PRIORITY = 1

VMEM_LIMIT = 56 * 1024 * 1024


def _params(sem, vmem=VMEM_LIMIT):
    return pltpu.CompilerParams(dimension_semantics=sem, vmem_limit_bytes=vmem)


def _unpack_pairs(u):
    lo = pltpu.bitcast(u << 16, F32)
    hi = pltpu.bitcast(u & jnp.uint32(0xFFFF0000), F32)
    return jnp.concatenate([lo, hi], axis=1)


def _pack_pairs(v):
    bits = pltpu.bitcast(v.astype(BF16).astype(F32), jnp.uint32)
    return (bits[:, :D_HALF] >> 16) | (bits[:, D_HALF:] & jnp.uint32(0xFFFF0000))


def _store_row_tiles(ref, packed):
    rows = packed.shape[0]
    for s in range(ROW_TILE):
        ref[pl.ds(s, rows, stride=ROW_TILE), :] = packed[:, s * LANES:(s + 1) * LANES]


def _load_row_tiles(ref, rows):
    return jnp.concatenate(
        [ref[pl.ds(s, rows, stride=ROW_TILE), :] for s in range(ROW_TILE)], axis=1)


def _moe_residual(x_ref, y0_ref, y1_ref, rg_ref):
    rg = rg_ref[...]
    rows = x_ref.shape[0]
    return (x_ref[...] + rg[:, 0:1] * _unpack_pairs(_load_row_tiles(y0_ref, rows))
            + rg[:, 1:2] * _unpack_pairs(_load_row_tiles(y1_ref, rows)))


def _project(x, g_ref, w_ref, z_ref):
    ms = jnp.mean(x * x, axis=-1, keepdims=True)
    h = ((x * lax.rsqrt(ms + EPS)) * g_ref[...]).astype(BF16)
    tn = 512
    for j in range(D_IN_PROJ // tn):
        cs = slice(j * tn, (j + 1) * tn)
        z_ref[:, cs] = jnp.dot(h, w_ref[:, cs], preferred_element_type=F32).astype(BF16)


def _inproj_body(x_ref, g_ref, w_ref, z_ref):
    _project(x_ref[...], g_ref, w_ref, z_ref)


def _inproj_moe_body(x_ref, y0_ref, y1_ref, rg_ref, g_ref, w_ref, z_ref, xo_ref):
    x = _moe_residual(x_ref, y0_ref, y1_ref, rg_ref)
    xo_ref[...] = x
    _project(x, g_ref, w_ref, z_ref)


def _inproj(l, x, g, w, moe=None):
    tm = 512
    row = pl.BlockSpec((tm, D_MODEL), lambda i: (i, 0))
    w_specs = [
        pl.BlockSpec((None, 1, D_MODEL), lambda i: (l, 0, 0)),
        pl.BlockSpec((None, D_MODEL, D_IN_PROJ), lambda i: (l, 0, 0), pipeline_mode=pl.Buffered(1)),
    ]
    z_spec = pl.BlockSpec((tm, D_IN_PROJ), lambda i: (i, 0))
    z_shape = jax.ShapeDtypeStruct((N_TOK, D_IN_PROJ), BF16)
    if moe is None:
        return pl.pallas_call(
            _inproj_body, grid=(N_TOK // tm,), in_specs=[row] + w_specs, out_specs=z_spec,
            out_shape=z_shape, compiler_params=_params(("parallel",)), name="inproj",
        )(x, g, w)
    yk, rg = moe
    return pl.pallas_call(
        _inproj_moe_body,
        grid=(N_TOK // tm,),
        in_specs=[
            row,
            pl.BlockSpec((None, tm * ROW_TILE, LANES), lambda i: (0, i, 0)),
            pl.BlockSpec((None, tm * ROW_TILE, LANES), lambda i: (1, i, 0)),
            pl.BlockSpec((tm, LANES), lambda i: (i, 0)),
        ] + w_specs,
        out_specs=[z_spec, row],
        out_shape=[z_shape, jax.ShapeDtypeStruct((N_TOK, D_MODEL), F32)],
        compiler_params=_params(("parallel",)),
        name="inproj_moe",
    )(x, yk, yk, rg, g, w)


def _fourier_consts():
    c = np.arange(HEAD_DIM)
    ang = 2.0 * np.pi * (np.outer(c, c) % HEAD_DIM) / HEAD_DIM
    cc, sc = np.cos(ang), np.sin(ang)
    scale = 1.0 / np.sqrt(SEQ * HEAD_DIM)
    cs = np.block([[cc, -sc], [sc, cc]]) * scale
    s1 = np.arange(FFT_P)
    m2 = np.zeros((FFT_Q, FFT_P, 2 * FFT_P), np.float64)
    for k2 in range(FFT_Q):
        k = FFT_Q * np.arange(FFT_P) + k2
        th = 2.0 * np.pi * (np.outer(k, s1) % SEQ) / SEQ
        m2[k2, :, :FFT_P] = np.cos(th)
        m2[k2, :, FFT_P:] = np.sin(th)
    return jnp.asarray(cs, BF16), jnp.asarray(m2, BF16)


def _fourier_body(z_ref, cs_ref, m2_ref, f_ref, y_scr, u_scr):
    rc = 64
    r = np.float32(np.sqrt(0.5))

    def chunk(ci, carry):
        r0 = pl.multiple_of(ci * rc, rc)
        zb = [z_ref[pl.ds(s2 * FFT_P + r0, rc), :].astype(F32) for s2 in range(FFT_Q)]
        e0, e1 = zb[0] + zb[4], zb[0] - zb[4]
        e2, e3 = zb[2] + zb[6], zb[2] - zb[6]
        o0, o1 = zb[1] + zb[5], zb[1] - zb[5]
        o2, o3 = zb[3] + zb[7], zb[3] - zb[7]
        p, q = (o1 - o3) * r, (o1 + o3) * r
        ee, oo = e0 + e2, o0 + o2
        ed, od = e0 - e2, o0 - o2
        zero = jnp.zeros_like(e0)
        ys = [(ee + oo, zero), (e1 + p, -e3 - q), (ed, -od), (e1 - p, e3 - q),
              (ee - oo, zero), (e1 - p, q - e3), (ed, od), (e1 + p, e3 + q)]
        for k2 in range(FFT_Q):
            re, im = ys[k2]
            rows = pl.ds(k2 * FFT_P + r0, rc)
            for h in range(FFT_HEADS):
                hs = slice(h * HEAD_DIM, (h + 1) * HEAD_DIM)
                y_scr[rows, 2 * h * HEAD_DIM:(2 * h + 1) * HEAD_DIM] = re[:, hs].astype(BF16)
                y_scr[rows, (2 * h + 1) * HEAD_DIM:(2 * h + 2) * HEAD_DIM] = im[:, hs].astype(BF16)
        return carry

    lax.fori_loop(0, FFT_P // rc, chunk, 0)

    for k2 in range(FFT_Q):
        for h in range(FFT_HEADS):
            yk = y_scr[k2 * FFT_P:(k2 + 1) * FFT_P, 2 * h * HEAD_DIM:(2 * h + 2) * HEAD_DIM]
            ab = jnp.dot(yk, cs_ref[...], preferred_element_type=F32)
            hs = slice(h * HEAD_DIM, (h + 1) * HEAD_DIM)
            u_scr[k2, 0:FFT_P, hs] = ab[:, :HEAD_DIM].astype(BF16)
            u_scr[k2, FFT_P:2 * FFT_P, hs] = ab[:, HEAD_DIM:].astype(BF16)

    for k2 in range(FFT_Q):
        res = jnp.dot(m2_ref[k2], u_scr[k2], preferred_element_type=F32)
        for h in range(FFT_HEADS):
            f_ref[h, pl.ds(k2, FFT_P, stride=FFT_Q), :] = res[:, h * HEAD_DIM:(h + 1) * HEAD_DIM]


def _fourier(z3, cs, m2):
    return pl.pallas_call(
        _fourier_body,
        grid=(BATCH, N_HEADS // FFT_HEADS),
        in_specs=[
            pl.BlockSpec((None, SEQ, FFT_W), lambda b, h: (b, 0, h)),
            pl.BlockSpec((2 * HEAD_DIM, 2 * HEAD_DIM), lambda b, h: (0, 0)),
            pl.BlockSpec((FFT_Q, FFT_P, 2 * FFT_P), lambda b, h: (0, 0, 0)),
        ],
        out_specs=pl.BlockSpec((None, FFT_HEADS, SEQ, HEAD_DIM), lambda b, h: (b, h, 0, 0)),
        out_shape=jax.ShapeDtypeStruct((BATCH, N_HEADS, SEQ, HEAD_DIM), F32),
        scratch_shapes=[
            pltpu.VMEM((SEQ, 2 * FFT_W), BF16),
            pltpu.VMEM((FFT_Q, 2 * FFT_P, FFT_W), BF16),
        ],
        compiler_params=_params(("parallel", "parallel")),
        name="fourier",
    )(z3, cs, m2)


def _gelu(x):
    return 0.5 * x * (1.0 + lax.erf(x * np.float32(np.sqrt(0.5))))


def _mixer_body(x_ref, f_ref, zu_ref, zv_ref, zb_ref, zc_ref, zval_ref, zp_ref,
                zc_prev, zval_prev, zp_prev, zc_next, zval_next, zp_next,
                wf_ref, ws_ref, bs_ref, cw_ref, wp_ref, ps_ref, on_ref, wo_ref,
                fg_ref, wr_ref, wrl_ref, br_ref, tri_ref,
                o_ref, hp_ref, ri_ref, rg_ref, cnt_ref, ext_a, ext_p, ybf, carry):
    i = pl.program_id(1)
    ts = x_ref.shape[0]
    keep_prev = (i > 0).astype(F32)
    keep_next = (i < pl.num_programs(1) - 1).astype(F32)

    def f32(ref):
        return ref[...].astype(F32)

    ext_a[0:HALO, :] = f32(zc_prev) * f32(zval_prev) * keep_prev
    ext_a[HALO:HALO + ts, :] = f32(zc_ref) * f32(zval_ref)
    ext_a[HALO + ts:2 * HALO + ts, :] = f32(zc_next) * f32(zval_next) * keep_next
    ext_p[0:HALO, :] = f32(zp_prev) * keep_prev
    ext_p[HALO:HALO + ts, :] = f32(zp_ref)
    ext_p[HALO + ts:2 * HALO + ts, :] = f32(zp_next) * keep_next

    def mix_rows(r0, nr):
        rs = slice(r0, r0 + nr)

        def norm_project(y, g):
            gs = slice(g * D_GROUP, (g + 1) * D_GROUP)
            ms = jnp.mean(y * y, axis=-1, keepdims=True)
            ybf[rs, gs] = ((y * lax.rsqrt(ms + EPS)) * on_ref[:, gs]).astype(BF16)

        spec = jnp.concatenate([f_ref[h, rs, :] for h in range(N_HEADS)], axis=1).astype(BF16)
        norm_project(jnp.dot(spec, wf_ref[...], preferred_element_type=F32), 0)

        gu = _gelu(zu_ref[rs, :].astype(F32))
        gv = _gelu(zv_ref[rs, :].astype(F32)).astype(BF16)
        cols = []
        for h in range(N_HEADS):
            hs = slice(h * HEAD_DIM, (h + 1) * HEAD_DIM)
            rows = []
            for n in range(nr // CHUNK):
                vv = gv[n * CHUNK:(n + 1) * CHUNK, hs]
                rows.append(jnp.dot(ws_ref[h], vv, preferred_element_type=F32) + bs_ref[h])
            cols.append(jnp.concatenate(rows, axis=0))
        norm_project(gu * jnp.concatenate(cols, axis=1), 1)

        conv = (cw_ref[0:1, :] * ext_a[HALO - 1 + r0:HALO - 1 + r0 + nr, :]
                + cw_ref[1:2, :] * ext_a[HALO + r0:HALO + r0 + nr, :]
                + cw_ref[2:3, :] * ext_a[HALO + 1 + r0:HALO + 1 + r0 + nr, :])
        norm_project(zb_ref[rs, :].astype(F32) * conv, 2)

        t = i * ts + r0 + lax.broadcasted_iota(jnp.int32, (nr, HEAD_DIM), 0)
        outs = []
        for g, w in enumerate(POOL_WINDOWS):
            gs = slice(g * HEAD_DIM, (g + 1) * HEAD_DIM)
            acc = ext_p[HALO - w // 2 + r0:HALO - w // 2 + r0 + nr, gs]
            for d in range(-w // 2 + 1, w // 2):
                acc = acc + ext_p[HALO + d + r0:HALO + d + r0 + nr, gs]
            cnt = (jnp.minimum(t + w // 2, SEQ) - jnp.maximum(t - w // 2, 0)).astype(F32)
            pg = acc / cnt - ext_p[HALO + r0:HALO + r0 + nr, gs]
            outs.append(jnp.dot(pg.astype(BF16), wp_ref[g], preferred_element_type=F32))
        norm_project(jnp.concatenate(outs, axis=1) * ps_ref[...], 3)

        o_ref[rs, :] = x_ref[rs, :] + jnp.dot(ybf[rs, :], wo_ref[...], preferred_element_type=F32)

    for r0 in range(0, ts, MIX_ROWS):
        mix_rows(r0, MIX_ROWS)

    @pl.when((pl.program_id(0) == 0) & (i == 0))
    def _():
        carry[...] = jnp.zeros_like(carry)

    _route(o_ref[...], fg_ref, wr_ref, wrl_ref, br_ref, tri_ref, hp_ref, ri_ref, rg_ref, cnt_ref,
           carry)


def _mixer(l, x3, f3, z3, wf, ws, bsb, cw, wp, ps, on, wo, fg, wr, wrl, br, tri):
    ts = tri.shape[0]
    nb8 = ts // HALO
    last8 = SEQ // HALO - 1
    n_i = SEQ // ts

    def tok(rows, width):
        return pl.BlockSpec((rows, width), lambda b, i: (b * n_i + i, 0))

    def col(j):
        return pl.BlockSpec((None, ts, D_GROUP), lambda b, i, j=j: (b, i, j))

    def prev(j):
        return pl.BlockSpec((None, HALO, D_GROUP),
                            lambda b, i, j=j: (b, jnp.maximum(i * nb8 - 1, 0), j))

    def nxt(j):
        return pl.BlockSpec((None, HALO, D_GROUP),
                            lambda b, i, j=j: (b, jnp.minimum((i + 1) * nb8, last8), j))

    def full(shape, **kw):
        return pl.BlockSpec((None,) + shape, lambda b, i: (l,) + (0,) * len(shape), **kw)

    return pl.pallas_call(
        _mixer_body,
        grid=(BATCH, SEQ // ts),
        in_specs=[
            pl.BlockSpec((None, ts, D_MODEL), lambda b, i: (b, i, 0)),
            pl.BlockSpec((None, N_HEADS, ts, HEAD_DIM), lambda b, i: (b, 0, i, 0)),
            col(1), col(2), col(3), col(4), col(5), col(6),
            prev(4), prev(5), prev(6), nxt(4), nxt(5), nxt(6),
            full((D_GROUP, D_GROUP)), full((N_HEADS, CHUNK, CHUNK)),
            full((N_HEADS, CHUNK, HEAD_DIM)), full((3, D_GROUP)),
            full((4, HEAD_DIM, HEAD_DIM)), full((1, D_GROUP)), full((1, D_MODEL)),
            full((D_MODEL, D_MODEL), pipeline_mode=pl.Buffered(1)),
            full((1, D_MODEL)), full((D_MODEL, LANES)), full((D_MODEL, LANES)), full((1, LANES)),
            pl.BlockSpec((ts, ts), lambda b, i: (0, 0)),
        ],
        out_specs=[
            pl.BlockSpec((None, ts, D_MODEL), lambda b, i: (b, i, 0)),
            tok(ts * ROW_TILE, LANES),
            pl.BlockSpec((ROUTE_ROWS, ts), lambda b, i: (0, b * n_i + i)),
            tok(ts, LANES),
            pl.BlockSpec((1, LANES), lambda b, i: (0, 0)),
        ],
        out_shape=[
            jax.ShapeDtypeStruct((BATCH, SEQ, D_MODEL), F32),
            jax.ShapeDtypeStruct((N_TOK * ROW_TILE, LANES), jnp.uint32),
            jax.ShapeDtypeStruct((ROUTE_ROWS, N_TOK), jnp.int32),
            jax.ShapeDtypeStruct((N_TOK, LANES), F32),
            jax.ShapeDtypeStruct((1, LANES), F32),
        ],
        scratch_shapes=[
            pltpu.VMEM((ts + 2 * HALO, D_GROUP), F32),
            pltpu.VMEM((ts + 2 * HALO, D_GROUP), F32),
            pltpu.VMEM((ts, D_MODEL), BF16),
            pltpu.VMEM((1, LANES), F32),
        ],
        compiler_params=_params(("arbitrary", "arbitrary")),
        name="mixer",
    )(x3, f3, z3, z3, z3, z3, z3, z3, z3, z3, z3, z3, z3, z3, wf, ws, bsb, cw, wp, ps, on, wo,
      fg, wr, wrl, br, tri)


def _route(x, g_ref, wr_ref, wrl_ref, br_ref, tri_ref, hp_ref, ri_ref, rg_ref, cnt_ref, carry):
    tr = x.shape[0]
    ms = jnp.mean(x * x, axis=-1, keepdims=True)
    h = (x * lax.rsqrt(ms + EPS)) * g_ref[...]

    _store_row_tiles(hp_ref, _pack_pairs(h))

    h_hi = h.astype(BF16)
    h_lo = (h - h_hi.astype(F32)).astype(BF16)
    logits = (jnp.dot(h_hi, wr_ref[...], preferred_element_type=F32)
              + jnp.dot(h_lo, wr_ref[...], preferred_element_type=F32)
              + jnp.dot(h_hi, wrl_ref[...], preferred_element_type=F32)) + br_ref[...]
    lane = lax.broadcasted_iota(jnp.int32, (tr, LANES), 1)
    neg = np.float32(-np.inf)

    def first_argmax(v, vmax):
        return jnp.min(jnp.where(v == vmax, lane, LANES), axis=-1, keepdims=True)

    mg = jnp.where(lane < N_EXPERT_GROUPS, logits, neg)
    gmax = jnp.max(mg, axis=-1, keepdims=True)
    grp = first_argmax(mg, gmax)
    p_grp = 1.0 / jnp.sum(jnp.exp(mg - gmax), axis=-1, keepdims=True)

    lo = ROUTE_OFF + EXPERTS_PER_GROUP * grp
    le = jnp.where(lane >= lo, jnp.where(lane < lo + EXPERTS_PER_GROUP, logits, neg), neg)
    emax = jnp.max(le, axis=-1, keepdims=True)
    i1 = first_argmax(le, emax)
    le2 = jnp.where(lane == i1, neg, le)
    emax2 = jnp.max(le2, axis=-1, keepdims=True)
    i2 = first_argmax(le2, emax2)
    e2 = jnp.exp(emax2 - emax)
    gate1 = p_grp / (1.0 + e2)
    gate2 = p_grp * e2 / (1.0 + e2)

    is1, is2 = lane == i1, lane == i2
    oh = jnp.where(is1, 1.0, jnp.where(is2, 1.0, 0.0))
    pref = jnp.dot(tri_ref[...], oh.astype(BF16), preferred_element_type=F32) + carry[...]
    r1 = jnp.sum(jnp.where(is1, pref, 0.0), axis=-1, keepdims=True).astype(jnp.int32)
    r2 = jnp.sum(jnp.where(is2, pref, 0.0), axis=-1, keepdims=True).astype(jnp.int32)
    carry[...] = carry[...] + jnp.sum(oh, axis=0, keepdims=True)
    cnt_ref[...] = carry[...]

    ri = jnp.where(lane == 0, i1 - ROUTE_OFF,
                   jnp.where(lane == 1, i2 - ROUTE_OFF,
                             jnp.where(lane == 2, r1, jnp.where(lane == 3, r2, 0))))
    ri_ref[...] = ri.T[0:ri_ref.shape[0], :]
    rg_ref[...] = jnp.where(lane == 0, gate1, jnp.where(lane == 1, gate2, 0.0))


def _invert_body(dest_ref, cnt_ref, rowv_ref, plan_ref):
    def zero_rows(lo, hi):
        def zero(r, c):
            rowv_ref[r] = 0
            return c

        lax.fori_loop(lo, hi, zero, 0)

    def per_expert(e, carry):
        cursor, nonempty = carry
        c = cnt_ref[e]
        nb = (c + (TM_MOE - 1)) // TM_MOE
        zero_rows(cursor * TM_MOE + c, (cursor + nb) * TM_MOE)

        def per_block(j, c2):
            plan_ref[PLAN_EXP + cursor + j] = e
            plan_ref[PLAN_VALID + cursor + j] = jnp.minimum(c - j * TM_MOE, TM_MOE)
            plan_ref[PLAN_SLOT + cursor + j] = nonempty & 1
            return c2

        lax.fori_loop(0, nb, per_block, 0)
        return cursor + nb, nonempty + jnp.where(nb > 0, 1, 0)

    n_active, _ = lax.fori_loop(0, N_EXPERTS, per_expert, (jnp.int32(0), jnp.int32(0)))
    plan_ref[PLAN_ACTIVE] = n_active
    zero_rows(n_active * TM_MOE, N_ROWS)

    def unused_block(b, c):
        plan_ref[PLAN_EXP + b] = plan_ref[PLAN_EXP + n_active - 1]
        plan_ref[PLAN_VALID + b] = 0
        plan_ref[PLAN_SLOT + b] = 0
        plan_ref[PLAN_NEXT + b] = -1
        return c

    lax.fori_loop(n_active, N_BLOCKS, unused_block, 0)

    def next_expert(j, carry):
        cur, nxt = carry
        b = n_active - 1 - j
        e = plan_ref[PLAN_EXP + b]
        nxt = jnp.where(e != cur, cur, nxt)
        plan_ref[PLAN_NEXT + b] = nxt
        return e, nxt

    lax.fori_loop(0, n_active, next_expert, (plan_ref[PLAN_EXP + n_active - 1], jnp.int32(-1)))

    unroll = 16

    def body(j, c):
        rows = [dest_ref[j * unroll + u] for u in range(unroll)]
        for u in range(unroll):
            rowv_ref[rows[u]] = j * unroll + u
        return c

    lax.fori_loop(0, 2 * N_TOK // unroll, body, 0)


def _invert(dest, counts):
    return pl.pallas_call(
        _invert_body,
        grid_spec=pltpu.PrefetchScalarGridSpec(
            num_scalar_prefetch=2,
            grid=(1,),
            in_specs=[],
            out_specs=[pl.BlockSpec(memory_space=pltpu.SMEM), pl.BlockSpec(memory_space=pltpu.SMEM)],
        ),
        out_shape=[jax.ShapeDtypeStruct((N_ROWS,), jnp.int32),
                   jax.ShapeDtypeStruct((PLAN_LEN,), jnp.int32)],
        compiler_params=_params(("arbitrary",)),
        name="invert",
    )(dest, counts)


def _moe_body(plan_ref, rowv_ref, hp_ref, wg_hbm, wu_hbm, wd_hbm, yk_ref, wg32, wu32, wd32,
              wg_ref, wu_ref, wd_ref, xbuf, ybuf, gsem, ssem, wsem, *, layer):
    i = pl.program_id(0)
    na = plan_ref[PLAN_ACTIVE]

    class _Field:
        def __init__(self, base):
            self.base = base

        def __getitem__(self, b):
            return plan_ref[self.base + b]

    be_ref, nv_ref, par_ref, nxe_ref = (_Field(b) for b in (PLAN_EXP, PLAN_VALID, PLAN_SLOT, PLAN_NEXT))

    def weight_copies(e, p):
        return [pltpu.make_async_copy(src.at[layer, e], dst.at[p], wsem.at[p])
                for src, dst in ((wg_hbm, wg32), (wu_hbm, wu32), (wd_hbm, wd32))]
    slot = lax.rem(i, 2)
    other = 1 - slot
    blk_rows = TM_MOE * ROW_TILE

    def tile_rows(r):
        start = r * ROW_TILE
        return pl.ds(start if isinstance(r, int) else pl.multiple_of(start, ROW_TILE), ROW_TILE)

    def gather_copy(blk, s, r):
        src = (rowv_ref[blk * TM_MOE + r] & (N_TOK - 1)) * ROW_TILE
        return pltpu.make_async_copy(hp_ref.at[pl.ds(pl.multiple_of(src, ROW_TILE), ROW_TILE), :],
                                     xbuf.at[s, tile_rows(r), :], gsem.at[s])

    def scatter_copy(blk, s, r):
        dst = rowv_ref[blk * TM_MOE + r] * ROW_TILE
        return pltpu.make_async_copy(ybuf.at[s, tile_rows(r), :],
                                     yk_ref.at[pl.ds(pl.multiple_of(dst, ROW_TILE), ROW_TILE), :],
                                     ssem.at[s])

    def wait_gather(s):
        pltpu.make_async_copy(hp_ref.at[pl.ds(0, blk_rows), :], xbuf.at[s], gsem.at[s]).wait()

    def wait_scatter(blk, s):
        n = nv_ref[blk]

        @pl.when(n == TM_MOE)
        def _():
            pltpu.make_async_copy(ybuf.at[s], yk_ref.at[pl.ds(0, blk_rows), :], ssem.at[s]).wait()

        @pl.when(n < TM_MOE)
        def _():
            lax.fori_loop(0, n, lambda r, c: (scatter_copy(blk, s, r).wait(), c)[1], 0)

    yslot = lax.rem(i, 3)
    yprev = lax.rem(i + 2, 3)
    nv_prev = nv_ref[jnp.maximum(i - 1, 0)]
    prev_full = (i >= 1) & (nv_prev == TM_MOE)

    def start(copy, priority=ROW_DMA_PRIORITY):
        copy.start(priority=priority)

    @pl.when(i == 0)
    def _():
        for c in weight_copies(be_ref[0], 0):
            c.start()
        lax.fori_loop(0, TM_MOE, lambda r, c: (start(gather_copy(0, 0, r)), c)[1], 0)

    @pl.when((i < na) & ((i == 0) | (be_ref[i] != be_ref[jnp.maximum(i - 1, 0)])))
    def _():
        p = par_ref[i]
        for c in weight_copies(be_ref[i], p):
            c.wait()

        @pl.when(nxe_ref[i] >= 0)
        def _():
            for c in weight_copies(nxe_ref[i], 1 - p):
                c.start()

        wg_ref[...] = wg32[p].astype(BF16)
        wu_ref[...] = wu32[p].astype(BF16)
        wd_ref[...] = wd32[p].astype(BF16)

    @pl.when((i >= 3) & (i < na))
    def _():
        wait_scatter(i - 3, yslot)

    @pl.when((i >= 1) & (i < na) & (nv_prev < TM_MOE))
    def _():
        lax.fori_loop(0, nv_prev, lambda r, c: (start(scatter_copy(i - 1, yprev, r)), c)[1], 0)

    def block_body(scatter_prev):
        wait_gather(slot)
        xu = _load_row_tiles(xbuf.at[slot], TM_MOE)
        lo = pltpu.bitcast(xu << 16, F32).astype(BF16)
        hi = pltpu.bitcast(xu & jnp.uint32(0xFFFF0000), F32).astype(BF16)
        xb = jnp.concatenate([lo, hi], axis=1)
        nxt = jnp.minimum(i + 1, na - 1)
        for r in range(TM_MOE):
            start(gather_copy(nxt, other, r), priority=r % 2)
            if scatter_prev:
                start(scatter_copy(i - 1, yprev, r), priority=(r + 1) % 2)
        g = jnp.dot(xb, wg_ref[...], preferred_element_type=F32)
        u = jnp.dot(xb, wu_ref[...], preferred_element_type=F32)
        act = (g * jax.nn.sigmoid(g) * u).astype(BF16)
        y = jnp.dot(act, wd_ref[...], preferred_element_type=F32)
        _store_row_tiles(ybuf.at[yslot], _pack_pairs(y))

    @pl.when((i < na) & prev_full)
    def _():
        block_body(True)

    @pl.when((i < na) & jnp.logical_not(prev_full))
    def _():
        block_body(False)

    @pl.when(i == na - 1)
    def _():
        lax.fori_loop(0, nv_ref[i], lambda r, c: (start(scatter_copy(i, yslot, r)), c)[1], 0)
        wait_gather(other)
        wait_scatter(i, yslot)

        @pl.when(i >= 1)
        def _():
            wait_scatter(i - 1, yprev)

        @pl.when(i >= 2)
        def _():
            wait_scatter(i - 2, lax.rem(i + 1, 3))


def _moe(l, plan, rowv, hp, wg, wu, wd):
    return pl.pallas_call(
        functools.partial(_moe_body, layer=l),
        grid_spec=pltpu.PrefetchScalarGridSpec(
            num_scalar_prefetch=2,
            grid=(N_BLOCKS,),
            in_specs=[pl.BlockSpec(memory_space=pl.ANY)] * 4,
            out_specs=pl.BlockSpec(memory_space=pl.ANY),
            scratch_shapes=[
                pltpu.VMEM((2, D_MODEL, D_EXPERT), F32),
                pltpu.VMEM((2, D_MODEL, D_EXPERT), F32),
                pltpu.VMEM((2, D_EXPERT, D_MODEL), F32),
                pltpu.VMEM((D_MODEL, D_EXPERT), BF16),
                pltpu.VMEM((D_MODEL, D_EXPERT), BF16),
                pltpu.VMEM((D_EXPERT, D_MODEL), BF16),
                pltpu.VMEM((2, TM_MOE * ROW_TILE, LANES), jnp.uint32),
                pltpu.VMEM((3, TM_MOE * ROW_TILE, LANES), jnp.uint32),
                pltpu.SemaphoreType.DMA((2,)),
                pltpu.SemaphoreType.DMA((3,)),
                pltpu.SemaphoreType.DMA((2,)),
            ],
        ),
        out_shape=jax.ShapeDtypeStruct((2 * N_TOK * ROW_TILE, LANES), jnp.uint32),
        compiler_params=_params(("arbitrary",), vmem=60 * 1024 * 1024),
        name="moe",
    )(plan, rowv, hp, wg, wu, wd)


def _final_body(x_ref, y0_ref, y1_ref, rg_ref, g_ref, o_ref):
    y = _moe_residual(x_ref, y0_ref, y1_ref, rg_ref)
    ms = jnp.mean(y * y, axis=-1, keepdims=True)
    o_ref[...] = (y * lax.rsqrt(ms + EPS)) * g_ref[...]


def _final(x, yk, rg, g):
    tm = 512
    row = pl.BlockSpec((tm, D_MODEL), lambda i: (i, 0))
    return pl.pallas_call(
        _final_body,
        grid=(N_TOK // tm,),
        in_specs=[
            row,
            pl.BlockSpec((None, tm * ROW_TILE, LANES), lambda i: (0, i, 0)),
            pl.BlockSpec((None, tm * ROW_TILE, LANES), lambda i: (1, i, 0)),
            pl.BlockSpec((tm, LANES), lambda i: (i, 0)),
            pl.BlockSpec((1, D_MODEL), lambda i: (0, 0)),
        ],
        out_specs=row,
        out_shape=jax.ShapeDtypeStruct((N_TOK, D_MODEL), F32),
        compiler_params=_params(("parallel",)),
        name="final",
    )(x, yk, yk, rg, g)


def _plan(ri, cnt):
    counts = cnt[0, ROUTE_OFF:ROUTE_OFF + N_EXPERTS].astype(jnp.int32)
    nblk = (counts + TM_MOE - 1) // TM_MOE
    pad_start = (jnp.cumsum(nblk) - nblk) * TM_MOE
    dest = ri[2:4]
    for e in range(N_EXPERTS):
        dest = dest + jnp.where(ri[0:2] == e, pad_start[e], 0)
    rowv, plan = _invert(dest.reshape(-1).astype(jnp.int32), counts)
    return plan, rowv


def kernel(x, mix_norm, w_in, w_fourier, w_spatial, b_spatial, conv_w, w_pool, pool_scale,
           out_norm, w_out, ffn_norm, w_group, b_group, w_router, b_router, w_gate, w_up,
           w_down, final_norm):
    cs, m2 = _fourier_consts()
    tr = 512
    tri = (jnp.arange(tr)[:, None] > jnp.arange(tr)[None, :]).astype(BF16)

    w_in_b = w_in.astype(BF16)
    w_f_b = w_fourier.astype(BF16)
    w_s_b = w_spatial.astype(BF16)
    w_p_b = w_pool.astype(BF16)
    w_o_b = w_out.astype(BF16)
    pad = LANES - N_EXPERT_GROUPS - N_EXPERTS
    wr = jnp.concatenate([w_group, w_router, jnp.zeros((DEPTH, D_MODEL, pad), F32)], axis=-1)
    wr_hi = wr.astype(BF16)
    wr_lo = (wr - wr_hi.astype(F32)).astype(BF16)
    br = jnp.concatenate([b_group, b_router, jnp.zeros((DEPTH, pad), F32)], axis=-1)[:, None, :]
    bsb = jnp.broadcast_to(b_spatial[:, :, :, None], (DEPTH, N_HEADS, CHUNK, HEAD_DIM))
    mix_g = mix_norm[:, None, :]
    ffn_g = ffn_norm[:, None, :]
    out_g = out_norm.reshape(DEPTH, 1, D_MODEL)
    pool_s = pool_scale[:, None, :]

    xt = x.reshape(N_TOK, D_MODEL)
    moe = None
    for l in range(DEPTH):
        if moe is None:
            z = _inproj(l, xt, mix_g, w_in_b)
        else:
            z, xt = _inproj(l, xt, mix_g, w_in_b, moe=moe)
        z3 = z.reshape(BATCH, SEQ, D_IN_PROJ)
        f3 = _fourier(z3, cs, m2)
        x3, hp, ri, rg, cnt = _mixer(l, xt.reshape(BATCH, SEQ, D_MODEL), f3, z3, w_f_b, w_s_b, bsb,
                                     conv_w, w_p_b, pool_s, out_g, w_o_b, ffn_g, wr_hi, wr_lo, br, tri)
        xt = x3.reshape(N_TOK, D_MODEL)
        plan = _plan(ri, cnt)
        yk = _moe(l, *plan, hp, w_gate, w_up, w_down)
        moe = (yk.reshape(2, N_TOK * ROW_TILE, LANES), rg)
    xt = _final(xt, moe[0], moe[1], final_norm[None, :])
    return xt.reshape(BATCH, SEQ, D_MODEL)
```
